```python
import math
import jax
import jax.numpy as jnp
from jax import lax
import numpy as np

D_MODEL = 4096
BATCH = 32
SEQ = 256
DEPTH = 2
DEC_BATCH = 8
DEC_SEQ = 2048
PAST_LEN = 512

GRID_W = 64
N_MIXERS = 2
N_SSD_LAYERS = (DEPTH + 1) // 2
N_HY_LAYERS = DEPTH // 2
N_SUB = 3
RMS_EPS = 1e-6
D_FF = 11008

SSD_EXPAND = 2
D_INNER = SSD_EXPAND * D_MODEL
SSD_HEAD_DIM = 64
SSD_HEADS = D_INNER // SSD_HEAD_DIM
SSD_GROUPS = 8
SSD_HPG = SSD_HEADS // SSD_GROUPS
D_STATE = 128
SSD_CONV = 5
SSD_CHUNK = 128
SSD_CONV_DIM = D_INNER + 2 * SSD_GROUPS * D_STATE
SSD_IN_DIM = D_INNER + SSD_CONV_DIM + 2 * SSD_HEADS

HY_ORDER = 2
HY_DIRS = 2
HY_SHORT = 3
HY_EMB = 33
HY_BANDS = (HY_EMB - 1) // 2
HY_FILTER_W = 64
HY_DECAY_TARGET = 1e-2
HY_FAST_DECAY = 0.3
HY_SLOW_DECAY = 1.5

kernel_name = "hybrid_ssd_hyena_diffusion_step"


def rms_norm(x, g):
    xf = x.astype(jnp.float32)
    y = xf * lax.rsqrt(jnp.mean(xf * xf, axis=-1, keepdims=True) + RMS_EPS)
    return (y * g.astype(jnp.float32)).astype(x.dtype)


def latent_pos_embed(n_tok, dtype):
    rows = n_tok // GRID_W
    r = jnp.repeat(jnp.arange(rows, dtype=jnp.float32), GRID_W)
    col = jnp.tile(jnp.arange(GRID_W, dtype=jnp.float32), rows)
    quarter = D_MODEL // 4
    omega = 1.0 / (10000.0 ** (jnp.arange(quarter, dtype=jnp.float32) / quarter))
    ar = r[:, None] * omega[None]
    ac = col[:, None] * omega[None]
    return jnp.concatenate([jnp.sin(ar), jnp.cos(ar), jnp.sin(ac), jnp.cos(ac)], axis=-1).astype(dtype)


def modulation(cvec, w_mod, b_mod):
    m = jax.nn.silu(cvec) @ w_mod + b_mod
    return m.reshape(cvec.shape[0], 3 * N_SUB, D_MODEL)


def pre_sub(h, mod, j, g_pre):
    shift = mod[:, 3 * j][:, None]
    scale = mod[:, 3 * j + 1][:, None]
    return rms_norm(h, g_pre) * (1.0 + scale) + shift


def post_sub(h, out, mod, j, g_post, weight):
    gate = mod[:, 3 * j + 2][:, None]
    return h + weight * gate * rms_norm(out, g_post)


def swiglu(u, w_gate, w_up, w_down):
    return (jax.nn.silu(u @ w_gate) * (u @ w_up)) @ w_down


def macaron_half(x, mod, j, slot, g_pre_l, g_post_l, w_gate, w_up, w_down):
    u = pre_sub(x, mod, j, g_pre_l[j])
    return post_sub(x, swiglu(u, w_gate[slot], w_up[slot], w_down[slot]), mod, j, g_post_l[j], 0.5)


def dwconv_centred(x, w, b):
    k_w = w.shape[0]
    half = k_w // 2
    n = x.shape[1]
    xp = jnp.pad(x, ((0, 0), (half, half), (0, 0)))
    y = b
    for k in range(k_w):
        y = y + xp[:, k:k + n, :] * w[k]
    return y


def ssd_scan(x, dt, a_head, bm, cm, h0):
    b, n = x.shape[0], x.shape[1]
    nc = n // SSD_CHUNK
    q = SSD_CHUNK
    xg = (x * dt[..., None]).reshape(b, nc, q, SSD_GROUPS, SSD_HPG, SSD_HEAD_DIM)
    a = (dt * a_head).reshape(b, nc, q, SSD_GROUPS, SSD_HPG)
    a_cs = jnp.cumsum(a, axis=2)
    bc = bm.reshape(b, nc, q, SSD_GROUPS, D_STATE)
    cc = cm.reshape(b, nc, q, SSD_GROUPS, D_STATE)
    seg = a_cs[:, :, :, None] - a_cs[:, :, None, :]
    mask = jnp.tril(jnp.ones((q, q), dtype=bool))[None, None, :, :, None, None]
    decay = jnp.exp(jnp.where(mask, seg, -jnp.inf))
    cb = jnp.einsum('bclgn,bcsgn->bclsg', cc, bc)
    y_diag = jnp.einsum('bclsg,bclsgr,bcsgrp->bclgrp', cb, decay, xg)
    decay_to_end = jnp.exp(a_cs[:, :, -1:] - a_cs)
    chunk_states = jnp.einsum('bclgn,bclgr,bclgrp->bcgrpn', bc, decay_to_end, xg)
    chunk_decay = jnp.exp(a_cs[:, :, -1])

    def step(hs, inp):
        s_c, d_c = inp
        return hs * d_c[..., None, None] + s_c, hs

    h0g = h0.reshape(b, SSD_GROUPS, SSD_HPG, SSD_HEAD_DIM, D_STATE)
    h_fin, h_in = lax.scan(step, h0g, (jnp.moveaxis(chunk_states, 1, 0), jnp.moveaxis(chunk_decay, 1, 0)))
    h_in = jnp.moveaxis(h_in, 0, 1)
    y_off = jnp.einsum('bclgn,bcgrpn,bclgr->bclgrp', cc, h_in, jnp.exp(a_cs))
    y = (y_diag + y_off).reshape(b, n, SSD_HEADS, SSD_HEAD_DIM)
    return y, h_fin.reshape(b, SSD_HEADS, SSD_HEAD_DIM, D_STATE)


def flip_seq(t):
    return jnp.flip(t, axis=1)


def ssd_mixer(u, h0_f, h0_b, w_in, conv_w, conv_b, dt_bias, a_log, d_skip, norm_w, w_out):
    f32 = jnp.float32
    b, n, _ = u.shape
    zxbcdt = u @ w_in
    z, xbc, dt_f, dt_b = jnp.split(zxbcdt, [D_INNER, D_INNER + SSD_CONV_DIM, D_INNER + SSD_CONV_DIM + SSD_HEADS], axis=-1)
    xbc = jax.nn.silu(dwconv_centred(xbc, conv_w, conv_b)).astype(f32)
    xs, bm, cm = jnp.split(xbc, [D_INNER, D_INNER + SSD_GROUPS * D_STATE], axis=-1)
    xs = xs.reshape(b, n, SSD_HEADS, SSD_HEAD_DIM)
    bm = bm.reshape(b, n, SSD_GROUPS, D_STATE)
    cm = cm.reshape(b, n, SSD_GROUPS, D_STATE)
    dt_f = jax.nn.softplus(dt_f.astype(f32) + dt_bias[0].astype(f32))
    dt_b = jax.nn.softplus(dt_b.astype(f32) + dt_bias[1].astype(f32))
    a = -jnp.exp(a_log.astype(f32))
    y_f, hf = ssd_scan(xs, dt_f, a[0], bm, cm, h0_f)
    y_b, hb = ssd_scan(flip_seq(xs), flip_seq(dt_b), a[1], flip_seq(bm), flip_seq(cm), h0_b)
    y = y_f + flip_seq(y_b) + xs * d_skip.astype(f32)[:, None]
    gs = D_INNER // SSD_GROUPS
    y = y.reshape(b, n, SSD_GROUPS, gs) * jax.nn.silu(z.astype(f32)).reshape(b, n, SSD_GROUPS, gs)
    y = y * lax.rsqrt(jnp.mean(y * y, axis=-1, keepdims=True) + RMS_EPS)
    y = y.reshape(b, n, D_INNER) * norm_w.astype(f32)
    return y.astype(u.dtype) @ w_out, jnp.stack([hf, hb], axis=1)


def hyena_filters(n, w1, b1, w2, b2, w3, b3, freq, w_out):
    f32 = jnp.float32
    t = jnp.linspace(0.0, 1.0, n, dtype=f32)[:, None]
    ang = (2.0 * math.pi * jnp.arange(n, dtype=f32) / n)[:, None] * jnp.linspace(1e-4, HY_BANDS - 1, HY_BANDS, dtype=f32)[None]
    feats = jnp.concatenate([t, jnp.cos(ang), -jnp.sin(ang)], axis=-1)
    fq = freq.astype(f32)
    hdn = jnp.sin(fq[0] * (feats @ w1.astype(f32) + b1.astype(f32)))
    hdn = jnp.sin(fq[1] * (hdn @ w2.astype(f32) + b2.astype(f32)))
    hdn = jnp.sin(fq[2] * (hdn @ w3.astype(f32) + b3.astype(f32)))
    k = (hdn @ w_out.astype(f32)).reshape(n, HY_ORDER, HY_DIRS, D_MODEL)
    deltas = jnp.abs(jnp.linspace(math.log(HY_DECAY_TARGET) / HY_SLOW_DECAY, math.log(HY_DECAY_TARGET) / HY_FAST_DECAY, D_MODEL, dtype=f32))
    k = k * jnp.exp(-t * deltas[None])[:, None, None, :]
    return k / (jnp.sum(jnp.abs(k), axis=0, keepdims=True) + RMS_EPS)


def hyena_mixer(u, w_in, b_in, short_w, short_b, f_w1, f_b1, f_w2, f_b2, f_w3, f_b3, f_freq, f_w_out, bias, w_out, b_out):
    f32 = jnp.float32
    n = u.shape[1]
    proj = dwconv_centred(u @ w_in + b_in, short_w, short_b)
    x1, x2, v = jnp.split(proj, 3, axis=-1)
    k = hyena_filters(n, f_w1, f_b1, f_w2, f_b2, f_w3, f_b3, f_freq, f_w_out)
    k_two = jnp.concatenate([k[:, :, 0], k[::-1, :, 1]], axis=0)
    k_f = jnp.fft.rfft(k_two, axis=0)
    z = v.astype(f32)
    for o, gate in enumerate((x1, x2)):
        z_f = jnp.fft.rfft(z, n=2 * n, axis=1)
        zc = jnp.fft.irfft(z_f * k_f[None, :, o], n=2 * n, axis=1)[:, :n]
        z = gate.astype(f32) * (zc + z * bias[o].astype(f32))
    return z.astype(u.dtype) @ w_out + b_out


def setup_inputs(seed: int = 0) -> dict:
    key = jax.random.key(seed)
    ks = iter(jax.random.split(key, 48))
    f32 = jnp.float32

    def nrm(shape, scale):
        return jax.random.normal(next(ks), shape, f32) * scale

    def uni(shape, lo, hi):
        return jax.random.uniform(next(ks), shape, f32, lo, hi)

    dt0 = jnp.exp(uni((N_SSD_LAYERS, 2, SSD_HEADS), math.log(1e-3), math.log(1e-1)))
    return {
        'x_prompt': nrm((BATCH, SEQ, D_MODEL), 1.0),
        'x_sample': nrm((DEC_BATCH, DEC_SEQ, D_MODEL), 1.0),
        'state_ssd': nrm((DEC_BATCH, N_SSD_LAYERS, 2, SSD_HEADS, SSD_HEAD_DIM, D_STATE), 0.1),
        'c': nrm((DEC_BATCH, D_MODEL), 1.0),
        'c_ctx': nrm((D_MODEL,), 1.0),
        'w_mod': nrm((DEPTH, D_MODEL, 3 * N_SUB * D_MODEL), D_MODEL ** -0.5),
        'b_mod': nrm((DEPTH, 3 * N_SUB * D_MODEL), 0.02),
        'g_pre': 1.0 + nrm((DEPTH, N_SUB, D_MODEL), 0.1),
        'g_post': 1.0 + nrm((DEPTH, N_SUB, D_MODEL), 0.1),
        'ffn_w_gate': nrm((DEPTH, 2, D_MODEL, D_FF), D_MODEL ** -0.5),
        'ffn_w_up': nrm((DEPTH, 2, D_MODEL, D_FF), D_MODEL ** -0.5),
        'ffn_w_down': nrm((DEPTH, 2, D_FF, D_MODEL), D_FF ** -0.5),
        'ssd_w_in': nrm((N_SSD_LAYERS, D_MODEL, SSD_IN_DIM), D_MODEL ** -0.5),
        'ssd_conv_w': nrm((N_SSD_LAYERS, SSD_CONV, SSD_CONV_DIM), SSD_CONV ** -0.5),
        'ssd_conv_b': nrm((N_SSD_LAYERS, SSD_CONV_DIM), 0.02),
        'ssd_dt_bias': dt0 + jnp.log(-jnp.expm1(-dt0)),
        'ssd_a_log': jnp.log(uni((N_SSD_LAYERS, 2, SSD_HEADS), 1.0, 16.0)),
        'ssd_d': 1.0 + nrm((N_SSD_LAYERS, SSD_HEADS), 0.1),
        'ssd_norm': 1.0 + nrm((N_SSD_LAYERS, D_INNER), 0.1),
        'ssd_w_out': nrm((N_SSD_LAYERS, D_INNER, D_MODEL), D_INNER ** -0.5),
        'hy_w_in': nrm((N_HY_LAYERS, D_MODEL, 3 * D_MODEL), D_MODEL ** -0.5),
        'hy_b_in': nrm((N_HY_LAYERS, 3 * D_MODEL), 0.02),
        'hy_short_w': nrm((N_HY_LAYERS, HY_SHORT, 3 * D_MODEL), HY_SHORT ** -0.5),
        'hy_short_b': nrm((N_HY_LAYERS, 3 * D_MODEL), 0.02),
        'hy_f_w1': nrm((N_HY_LAYERS, HY_EMB, HY_FILTER_W), HY_EMB ** -0.5),
        'hy_f_b1': nrm((N_HY_LAYERS, HY_FILTER_W), 0.1),
        'hy_f_w2': nrm((N_HY_LAYERS, HY_FILTER_W, HY_FILTER_W), HY_FILTER_W ** -0.5),
        'hy_f_b2': nrm((N_HY_LAYERS, HY_FILTER_W), 0.1),
        'hy_f_w3': nrm((N_HY_LAYERS, HY_FILTER_W, HY_FILTER_W), HY_FILTER_W ** -0.5),
        'hy_f_b3': nrm((N_HY_LAYERS, HY_FILTER_W), 0.1),
        'hy_f_freq': 1.0 + nrm((N_HY_LAYERS, 3, HY_FILTER_W), 0.1),
        'hy_f_w_out': nrm((N_HY_LAYERS, HY_FILTER_W, HY_ORDER * HY_DIRS * D_MODEL), HY_FILTER_W ** -0.5),
        'hy_bias': nrm((N_HY_LAYERS, HY_ORDER, D_MODEL), 0.5),
        'hy_w_out': nrm((N_HY_LAYERS, D_MODEL, D_MODEL), D_MODEL ** -0.5),
        'hy_b_out': nrm((N_HY_LAYERS, D_MODEL), 0.02),
    }


def reference(x_prompt, x_sample, state_ssd, c, c_ctx, w_mod, b_mod, g_pre, g_post,
              ffn_w_gate, ffn_w_up, ffn_w_down,
              ssd_w_in, ssd_conv_w, ssd_conv_b, ssd_dt_bias, ssd_a_log, ssd_d, ssd_norm, ssd_w_out,
              hy_w_in, hy_b_in, hy_short_w, hy_short_b, hy_f_w1, hy_f_b1, hy_f_w2, hy_f_b2,
              hy_f_w3, hy_f_b3, hy_f_freq, hy_f_w_out, hy_bias, hy_w_out, hy_b_out):
    h = x_prompt
    g = x_sample + latent_pos_embed(x_sample.shape[1], x_sample.dtype)[None]
    new_states = []
    for i in range(DEPTH):
        kind = i % N_MIXERS
        li = i // N_MIXERS
        m_ctx = modulation(c_ctx[None], w_mod[i], b_mod[i])
        m_lat = modulation(c, w_mod[i], b_mod[i])
        h = macaron_half(h, m_ctx, 0, 0, g_pre[i], g_post[i], ffn_w_gate[i], ffn_w_up[i], ffn_w_down[i])
        g = macaron_half(g, m_lat, 0, 0, g_pre[i], g_post[i], ffn_w_gate[i], ffn_w_up[i], ffn_w_down[i])
        u_h = pre_sub(h, m_ctx, 1, g_pre[i, 1])
        u_g = pre_sub(g, m_lat, 1, g_pre[i, 1])
        if kind == 0:
            zero = jnp.zeros((h.shape[0], SSD_HEADS, SSD_HEAD_DIM, D_STATE), jnp.float32)
            o_h, st = ssd_mixer(u_h, zero, zero, ssd_w_in[li], ssd_conv_w[li], ssd_conv_b[li], ssd_dt_bias[li],
                                ssd_a_log[li], ssd_d[li], ssd_norm[li], ssd_w_out[li])
            cached = state_ssd[:, li].astype(jnp.float32)
            o_g, _ = ssd_mixer(u_g, cached[:, 0], cached[:, 1], ssd_w_in[li], ssd_conv_w[li], ssd_conv_b[li],
                               ssd_dt_bias[li], ssd_a_log[li], ssd_d[li], ssd_norm[li], ssd_w_out[li])
            new_states.append(st.astype(x_prompt.dtype))
        else:
            o_h = hyena_mixer(u_h, hy_w_in[li], hy_b_in[li], hy_short_w[li], hy_short_b[li], hy_f_w1[li], hy_f_b1[li],
                              hy_f_w2[li], hy_f_b2[li], hy_f_w3[li], hy_f_b3[li], hy_f_freq[li], hy_f_w_out[li],
                              hy_bias[li], hy_w_out[li], hy_b_out[li])
            o_g = hyena_mixer(u_g, hy_w_in[li], hy_b_in[li], hy_short_w[li], hy_short_b[li], hy_f_w1[li], hy_f_b1[li],
                              hy_f_w2[li], hy_f_b2[li], hy_f_w3[li], hy_f_b3[li], hy_f_freq[li], hy_f_w_out[li],
                              hy_bias[li], hy_w_out[li], hy_b_out[li])
        h = post_sub(h, o_h, m_ctx, 1, g_post[i, 1], 1.0)
        g = post_sub(g, o_g, m_lat, 1, g_post[i, 1], 1.0)
        h = macaron_half(h, m_ctx, 2, 1, g_pre[i], g_post[i], ffn_w_gate[i], ffn_w_up[i], ffn_w_down[i])
        g = macaron_half(g, m_lat, 2, 1, g_pre[i], g_post[i], ffn_w_gate[i], ffn_w_up[i], ffn_w_down[i])
    new_state_ssd = jnp.stack(new_states, axis=1)
    return (h, g, new_state_ssd)
```

```python
import functools
import math

import jax
import jax.numpy as jnp
from jax import lax
from jax.experimental import pallas as pl
from jax.experimental.pallas import tpu as pltpu

F32 = jnp.float32
BF16 = jnp.bfloat16
HIGHEST = lax.Precision.HIGHEST

RMS_EPS = 1e-6
N_SUB = 3
GRID_W = 64

SSD_HEAD_DIM = 64
SSD_GROUPS = 8
D_STATE = 128
SSD_CHUNK = 128

HY_BANDS = 16
HY_DECAY_TARGET = 1e-2
HY_FAST_DECAY = 0.3
HY_SLOW_DECAY = 1.5

_V7X_VMEM_BYTES = 64 * 1024 * 1024
_VMEM_LIMIT = _V7X_VMEM_BYTES - 8 * 1024 * 1024
_LANE = 128
_FF_ALIGN = 1024


def _params(*semantics):
    return pltpu.CompilerParams(dimension_semantics=semantics, vmem_limit_bytes=_VMEM_LIMIT)


def _tile(dim, target):
    if dim <= target:
        return dim
    for t in range(target - target % _LANE, 0, -_LANE):
        if dim % t == 0:
            return t
    raise ValueError(f"no {_LANE}-aligned tile <= {target} divides {dim}")


def _nt_dot(a, b, **kw):
    return lax.dot_general(a, b, (((1,), (1,)), ((), ())), preferred_element_type=F32, **kw)


def _silu(x):
    return x * jax.nn.sigmoid(x)


def _softplus(x):
    return jnp.maximum(x, 0.0) + jnp.log1p(jnp.exp(-jnp.abs(x)))


def _mod_kernel(c_ref, w_ref, b_ref, o_ref):
    a = _silu(c_ref[...]).astype(BF16)
    o_ref[...] = jnp.dot(a, w_ref[...].astype(BF16), preferred_element_type=F32) + b_ref[...]


def _modulation(cvec, w_mod, b_mod, tn=512):
    depth, d, n = w_mod.shape
    r = cvec.shape[0]
    return pl.pallas_call(
        _mod_kernel,
        out_shape=jax.ShapeDtypeStruct((depth, r, n), F32),
        grid=(depth, n // tn),
        in_specs=[
            pl.BlockSpec((r, d), lambda l, j: (0, 0)),
            pl.BlockSpec((None, d, tn), lambda l, j: (l, 0, j)),
            pl.BlockSpec((None, 1, tn), lambda l, j: (l, 0, j)),
        ],
        out_specs=pl.BlockSpec((None, r, tn), lambda l, j: (l, 0, j)),
        compiler_params=_params("parallel", "parallel"),
        name="modulation",
    )(cvec, w_mod, b_mod.reshape(depth, 1, n))


def _rms(v, g):
    return v * lax.rsqrt(jnp.mean(v * v, axis=-1, keepdims=True) + RMS_EPS) * g


def _post_pre_kernel(*refs, has_pos, has_post, has_pre, weight):
    refs = list(refs)
    x = refs.pop(0)[...]
    if has_pos:
        x = x + refs.pop(0)[...]
    if has_post:
        o_ref, gate_ref, gpost_ref = refs.pop(0), refs.pop(0), refs.pop(0)
        x = x + weight * gate_ref[...] * _rms(o_ref[...], gpost_ref[...])
    if has_pre:
        shift_ref, scale_ref, gpre_ref = refs.pop(0), refs.pop(0), refs.pop(0)
        u = _rms(x, gpre_ref[...]) * (1.0 + scale_ref[...]) + shift_ref[...]
    if has_pos or has_post:
        refs.pop(0)[...] = x
    if has_pre:
        refs.pop(0)[...] = u.astype(BF16)


def _post_pre(x, seq_len, *, pos=None, post=None, pre=None, tm=256):
    t, d = x.shape
    tm = min(tm, seq_len)
    assert seq_len % tm == 0 and t % tm == 0

    def mod_spec(m):
        if m.shape[0] == 1:
            return pl.BlockSpec((None, 1, d), lambda i: (0, 0, 0))
        return pl.BlockSpec((None, 1, d), lambda i: ((i * tm) // seq_len, 0, 0))

    row_spec = pl.BlockSpec((tm, d), lambda i: (i, 0))
    vec_spec = pl.BlockSpec((1, d), lambda i: (0, 0))
    args, in_specs = [x], [row_spec]
    if pos is not None:
        per_seq = seq_len // tm
        args.append(pos)
        in_specs.append(pl.BlockSpec((tm, d), lambda i: (i % per_seq, 0)))
    weight = 0.0
    if post is not None:
        o, gate, g_post, weight = post
        args += [o, gate, g_post]
        in_specs += [row_spec, mod_spec(gate), vec_spec]
    if pre is not None:
        shift, scale, g_pre = pre
        args += [shift, scale, g_pre]
        in_specs += [mod_spec(shift), mod_spec(scale), vec_spec]
    out_shape, out_specs = [], []
    new_x = pos is not None or post is not None
    if new_x:
        out_shape.append(jax.ShapeDtypeStruct((t, d), F32))
        out_specs.append(row_spec)
    if pre is not None:
        out_shape.append(jax.ShapeDtypeStruct((t, d), BF16))
        out_specs.append(row_spec)
    outs = pl.pallas_call(
        functools.partial(_post_pre_kernel, has_pos=pos is not None, has_post=post is not None,
                          has_pre=pre is not None, weight=weight),
        out_shape=out_shape,
        grid=(t // tm,),
        in_specs=in_specs,
        out_specs=out_specs,
        compiler_params=_params("parallel"),
        name="post_pre",
    )(*args)
    outs = list(outs)
    x_new = outs.pop(0) if new_x else None
    u = outs.pop(0) if pre is not None else None
    return x_new, u


def _mm_kernel(*refs, nk, has_bias):
    refs = list(refs)
    x_ref, w_ref = refs.pop(0), refs.pop(0)
    b_ref = refs.pop(0) if has_bias else None
    o_ref = refs.pop(0)
    part = jnp.dot(x_ref[...], w_ref[...], preferred_element_type=F32)

    def finish(acc):
        if has_bias:
            acc = acc + b_ref[...]
        o_ref[...] = acc.astype(o_ref.dtype)

    if nk == 1:
        finish(part)
        return
    acc_ref = refs.pop(0)
    k = pl.program_id(2)

    @pl.when(k == 0)
    def _():
        acc_ref[...] = part

    @pl.when(k > 0)
    def _():
        acc_ref[...] += part

    @pl.when(k == nk - 1)
    def _():
        finish(acc_ref[...])


def _matmul(x, w, bias=None, out_dtype=F32, tm=1024, tn=1024, tk=4096):
    t, kdim = x.shape
    n = w.shape[1]
    tm, tn, tk = _tile(t, tm), _tile(n, tn), _tile(kdim, tk)
    nk = kdim // tk
    args = [x, w]
    in_specs = [pl.BlockSpec((tm, tk), lambda i, j, k: (i, k)),
                pl.BlockSpec((tk, tn), lambda i, j, k: (k, j))]
    if bias is not None:
        args.append(bias.reshape(1, n).astype(F32))
        in_specs.append(pl.BlockSpec((1, tn), lambda i, j, k: (0, j)))
    return pl.pallas_call(
        functools.partial(_mm_kernel, nk=nk, has_bias=bias is not None),
        out_shape=jax.ShapeDtypeStruct((t, n), out_dtype),
        grid=(t // tm, n // tn, nk),
        in_specs=in_specs,
        out_specs=pl.BlockSpec((tm, tn), lambda i, j, k: (i, j)),
        scratch_shapes=[pltpu.VMEM((tm, tn), F32)] if nk > 1 else [],
        compiler_params=_params("parallel", "parallel", "arbitrary"),
        name="matmul",
    )(*args)


def _gate_up_kernel(u_ref, wg_ref, wu_ref, o_ref):
    u = u_ref[...]
    g = jnp.dot(u, wg_ref[...], preferred_element_type=F32)
    v = jnp.dot(u, wu_ref[...], preferred_element_type=F32)
    o_ref[...] = (_silu(g) * v).astype(o_ref.dtype)


def _gate_up(u, w_gate, w_up, tm=1024, tn=512):
    t, d = u.shape
    n = w_gate.shape[1]
    tm = min(tm, t)
    assert t % tm == 0 and n % tn == 0
    return pl.pallas_call(
        _gate_up_kernel,
        out_shape=jax.ShapeDtypeStruct((t, n), BF16),
        grid=(t // tm, n // tn),
        in_specs=[pl.BlockSpec((tm, d), lambda i, j: (i, 0)),
                  pl.BlockSpec((d, tn), lambda i, j: (0, j)),
                  pl.BlockSpec((d, tn), lambda i, j: (0, j))],
        out_specs=pl.BlockSpec((tm, tn), lambda i, j: (i, j)),
        compiler_params=_params("parallel", "parallel"),
        name="gate_up",
    )(u, w_gate, w_up)


_HALO = 8
_CONV_ROWS = 256
_HY_ROWS = 512


def _dwconv_kernel(x_ref, w_ref, b_ref, o_ref, pad_ref, *, act):
    n, _ = x_ref.shape
    k_w = w_ref.shape[0]
    rows = min(_CONV_ROWS, n)
    zeros = jnp.zeros((_HALO, pad_ref.shape[1]), F32)
    pad_ref[pl.ds(0, _HALO), :] = zeros
    pad_ref[pl.ds(_HALO + n, _HALO), :] = zeros
    pad_ref[pl.ds(_HALO, n), :] = x_ref[...].astype(F32)

    def body(i, carry):
        r0 = pl.multiple_of(i * rows, rows)
        win = pad_ref[pl.ds(r0, rows + 2 * _HALO), :]
        y = b_ref[...]
        for k in range(k_w):
            d = k - k_w // 2
            shifted = win if d == 0 else pltpu.roll(win, (-d) % win.shape[0], 0)
            y = y + shifted[_HALO:_HALO + rows, :] * w_ref[pl.ds(k, 1), :]
        o_ref[pl.ds(r0, rows), :] = act(y).astype(o_ref.dtype)
        return carry

    lax.fori_loop(0, n // rows, body, 0)


def _dwconv(x, col0, conv_w, conv_b, act, tc=512):
    b, l, _ = x.shape
    k_w, c = conv_w.shape
    assert k_w // 2 <= _HALO and l % min(_CONV_ROWS, l) == 0 and col0 % tc == 0 and c % tc == 0
    off = col0 // tc
    return pl.pallas_call(
        functools.partial(_dwconv_kernel, act=act),
        out_shape=jax.ShapeDtypeStruct((b, l, c), BF16),
        grid=(b, c // tc),
        in_specs=[pl.BlockSpec((None, l, tc), lambda i, j: (i, 0, off + j)),
                  pl.BlockSpec((k_w, tc), lambda i, j: (0, j)),
                  pl.BlockSpec((1, tc), lambda i, j: (0, j))],
        out_specs=pl.BlockSpec((None, l, tc), lambda i, j: (i, 0, j)),
        scratch_shapes=[pltpu.VMEM((l + 2 * _HALO, tc), F32)],
        compiler_params=_params("parallel", "parallel"),
        name="dwconv",
    )(x, conv_w.astype(F32), conv_b.astype(F32).reshape(1, c))


def _ssd_scan_kernel(*refs, has_h0, emit_state, hpg, p_dim):
    refs = list(refs)
    xs_ref, b_ref, c_ref, z_ref, dt_ref, dtb_ref, a_ref, dsk_ref, nw_ref = refs[:9]
    refs = refs[9:]
    h0_ref = refs.pop(0) if has_h0 else None
    y_ref = refs.pop(0)
    st_out_ref = refs.pop(0) if emit_state else None
    ybuf_ref, stf_ref, stb_ref = refs

    q = SSD_CHUNK
    seq, gp = xs_ref.shape
    nc = seq // q
    t_i = lax.broadcasted_iota(jnp.int32, (q, q), 0)
    s_i = lax.broadcasted_iota(jnp.int32, (q, q), 1)
    masks = (s_i <= t_i, s_i >= t_i)
    tris = (masks[0].astype(F32), masks[1].astype(F32))
    eye = (lax.broadcasted_iota(jnp.int32, (hpg, hpg), 0)
           == lax.broadcasted_iota(jnp.int32, (hpg, hpg), 1)).astype(F32)
    expand = (lax.broadcasted_iota(jnp.int32, (hpg, gp), 1) // p_dim
              == lax.broadcasted_iota(jnp.int32, (hpg, gp), 0)).astype(F32)
    expand_bf = expand.astype(BF16)
    pair_lane = lax.broadcasted_iota(jnp.int32, (q, 2 * p_dim), 1)

    def chunk(r0, d, st_ref):
        rows = pl.ds(r0, q)
        hs = slice(d * hpg, (d + 1) * hpg)
        xs = xs_ref[rows, :].astype(F32)
        bm, cm = b_ref[rows, :], c_ref[rows, :]
        dt = _softplus(dt_ref[rows, :][:, hs] + dtb_ref[...][:, hs])
        a = dt * a_ref[...][:, hs]
        cs = jnp.dot(tris[d], a, precision=HIGHEST, preferred_element_type=F32)
        cs_t = _nt_dot(eye, cs, precision=HIGHEST)
        end = cs[q - 1:q, :] if d == 0 else cs[0:1, :]
        dte = jnp.exp(end - cs)
        factors = jnp.concatenate([dt, dt * dte, jnp.exp(cs)], axis=0).astype(BF16)
        fx = jnp.dot(factors, expand_bf, preferred_element_type=F32)
        xg = (xs * fx[:q]).astype(BF16)
        xd = (xs * fx[q:2 * q]).astype(BF16)
        chunk_decay = jnp.dot(jnp.broadcast_to(jnp.exp(end), (8, hpg)), expand,
                              precision=HIGHEST, preferred_element_type=F32)[0:1]
        cb = _nt_dot(cm, bm)
        pieces = []
        for p in range(hpg // 2):
            xg_pair = xg[:, 2 * p * p_dim:(2 * p + 2) * p_dim]
            outs = []
            for r in (2 * p, 2 * p + 1):
                seg = cs[:, r:r + 1] - cs_t[r:r + 1, :]
                m = (cb * jnp.exp(jnp.where(masks[d], seg, -jnp.inf))).astype(BF16)
                outs.append(jnp.dot(m, xg_pair, preferred_element_type=F32))
            pieces.append(jnp.where(pair_lane < p_dim, outs[0], outs[1]))
        y_diag = jnp.concatenate(pieces, axis=1)
        st = st_ref[...]
        y_off = jnp.dot(cm, st.astype(BF16), preferred_element_type=F32) * fx[2 * q:]
        b_t = bm.astype(F32).T.astype(BF16)
        st_ref[...] = st * chunk_decay + jnp.dot(b_t, xd, preferred_element_type=F32)
        return y_diag + y_off, xs

    if has_h0:
        stf_ref[...] = h0_ref[0]
        stb_ref[...] = h0_ref[1]
    else:
        stf_ref[...] = jnp.zeros_like(stf_ref)
        stb_ref[...] = jnp.zeros_like(stb_ref)

    def bwd_body(i, carry):
        r0 = pl.multiple_of((nc - 1 - i) * q, q)
        y, _ = chunk(r0, 1, stb_ref)
        ybuf_ref[pl.ds(r0, q), :] = y
        return carry

    lax.fori_loop(0, nc, bwd_body, 0)

    def fwd_body(i, carry):
        r0 = pl.multiple_of(i * q, q)
        y, xs = chunk(r0, 0, stf_ref)
        y = y + ybuf_ref[pl.ds(r0, q), :] + xs * dsk_ref[...]
        y = y * _silu(z_ref[pl.ds(r0, q), :].astype(F32))
        y_ref[pl.ds(r0, q), :] = _rms(y, nw_ref[...]).astype(y_ref.dtype)
        return carry

    lax.fori_loop(0, nc, fwd_body, 0)

    if emit_state:
        st_out_ref[0] = stf_ref[...]
        st_out_ref[1] = stb_ref[...]


def _ssd_scan(zxbc, xbc, dt, dt_bias, a_head, d_skip, norm_w, h0, emit_state, d_inner):
    b, l, _ = zxbc.shape
    g = SSD_GROUPS
    gp = d_inner // g
    n = D_STATE
    hpg = gp // SSD_HEAD_DIM
    assert gp % _LANE == 0 and l % SSD_CHUNK == 0 and n == _LANE
    xoff = d_inner // n
    in_specs = [
        pl.BlockSpec((None, l, gp), lambda i, j: (i, 0, j)),
        pl.BlockSpec((None, l, n), lambda i, j: (i, 0, xoff + j)),
        pl.BlockSpec((None, l, n), lambda i, j: (i, 0, xoff + g + j)),
        pl.BlockSpec((None, l, gp), lambda i, j: (i, 0, j)),
        pl.BlockSpec((None, None, l, 2 * hpg), lambda i, j: (i, j, 0, 0)),
        pl.BlockSpec((None, 1, 2 * hpg), lambda i, j: (j, 0, 0)),
        pl.BlockSpec((None, 1, 2 * hpg), lambda i, j: (j, 0, 0)),
        pl.BlockSpec((1, gp), lambda i, j: (0, j)),
        pl.BlockSpec((1, gp), lambda i, j: (0, j)),
    ]
    args = [xbc, xbc, xbc, zxbc, dt, dt_bias, a_head, d_skip, norm_w]
    state_spec = pl.BlockSpec((None, 2, None, n, gp), lambda i, j: (i, 0, j, 0, 0))
    if h0 is not None:
        in_specs.append(state_spec)
        args.append(h0)
    out_shape = [jax.ShapeDtypeStruct((b, l, d_inner), BF16)]
    out_specs = [pl.BlockSpec((None, l, gp), lambda i, j: (i, 0, j))]
    if emit_state:
        out_shape.append(jax.ShapeDtypeStruct((b, 2, g, n, gp), F32))
        out_specs.append(state_spec)
    outs = pl.pallas_call(
        functools.partial(_ssd_scan_kernel, has_h0=h0 is not None, emit_state=emit_state,
                          hpg=hpg, p_dim=SSD_HEAD_DIM),
        out_shape=out_shape,
        grid=(b, g),
        in_specs=in_specs,
        out_specs=out_specs,
        scratch_shapes=[pltpu.VMEM((l, gp), F32), pltpu.VMEM((n, gp), F32), pltpu.VMEM((n, gp), F32)],
        compiler_params=_params("parallel", "parallel"),
        name="ssd_scan",
    )(*args)
    return outs[0], (outs[1] if emit_state else None)


def _ssd_mixer(u, bsz, seq, h0, emit_state, w_in_main, w_in_dt, conv_w, conv_b, dt_bias, a_log, d_skip,
               norm_w, w_out):
    d_inner = w_out.shape[0]
    heads = d_inner // SSD_HEAD_DIM
    g = SSD_GROUPS
    hpg = heads // g
    gp = d_inner // g
    zxbc = _matmul(u, w_in_main, out_dtype=BF16).reshape(bsz, seq, -1)
    dt = _matmul(u, w_in_dt, out_dtype=F32)
    dt = dt.reshape(bsz, seq, 2, g, hpg).transpose(0, 3, 1, 2, 4).reshape(bsz, g, seq, 2 * hpg)
    per_group = lambda v: v.astype(F32).reshape(2, g, hpg).transpose(1, 0, 2).reshape(g, 1, 2 * hpg)
    xbc = _dwconv(zxbc, d_inner, conv_w, conv_b, _silu)
    if h0 is not None:
        h0 = h0.astype(F32).reshape(bsz, 2, g, hpg, SSD_HEAD_DIM, D_STATE)
        h0 = h0.transpose(0, 1, 2, 5, 3, 4).reshape(bsz, 2, g, D_STATE, gp)
    y, st = _ssd_scan(zxbc, xbc, dt, per_group(dt_bias), per_group(-jnp.exp(a_log.astype(F32))),
                      jnp.repeat(d_skip.astype(F32), SSD_HEAD_DIM).reshape(1, d_inner),
                      norm_w.astype(F32).reshape(1, d_inner), h0, emit_state, d_inner)
    out = _matmul(y.reshape(bsz * seq, d_inner), w_out, out_dtype=F32)
    if st is not None:
        st = st.reshape(bsz, 2, g, D_STATE, hpg, SSD_HEAD_DIM).transpose(0, 1, 2, 4, 5, 3)
        st = st.reshape(bsz, 2, heads, SSD_HEAD_DIM, D_STATE)
    return out, st


def _hy_mlp_kernel(bands_ref, w1t_ref, w1c_ref, w1s_ref, b1_ref, w2_ref, b2_ref, w3_ref, b3_ref, fq_ref,
                   o_ref, *, n):
    idx = lax.broadcasted_iota(jnp.int32, (n, 1), 0).astype(F32)
    hdot = functools.partial(jnp.dot, precision=HIGHEST, preferred_element_type=F32)
    for direction in range(2):
        pos = idx if direction == 0 else (n - 1.0) - idx
        t = pos / (n - 1.0)
        ang = (2.0 * math.pi * pos / n) * bands_ref[...]
        h = t * w1t_ref[...] + hdot(jnp.cos(ang), w1c_ref[...]) - hdot(jnp.sin(ang), w1s_ref[...])
        h = jnp.sin(fq_ref[0:1, :] * (h + b1_ref[...]))
        h = jnp.sin(fq_ref[1:2, :] * (hdot(h, w2_ref[...]) + b2_ref[...]))
        h = jnp.sin(fq_ref[2:3, :] * (hdot(h, w3_ref[...]) + b3_ref[...]))
        o_ref[direction] = h


def _hy_filter_kernel(h_ref, w_ref, delta_ref, o_ref, *, n):
    direction = pl.program_id(0) % 2
    idx = lax.broadcasted_iota(jnp.int32, (n, 1), 0).astype(F32)
    pos = jnp.where(direction == 0, idx, (n - 1.0) - idx)
    t = pos / (n - 1.0)
    k = jnp.dot(h_ref[...], w_ref[...], precision=HIGHEST, preferred_element_type=F32)
    k = k * jnp.exp(-t * delta_ref[...])
    o_ref[...] = k / (jnp.sum(jnp.abs(k), axis=0, keepdims=True) + RMS_EPS)


def _hy_spectrum_kernel(kf_ref, kb_ref, c_ref, s_ref, kp_ref, kq_ref, kn_ref, kf16_ref, kb16_ref, *, n, rchunk):
    dot = functools.partial(jnp.dot, preferred_element_type=F32)
    chunks = [slice(r0, r0 + rchunk) for r0 in range(0, n, rchunk)]
    alt = _alt_sign(0, rchunk)
    sign_n = 1.0 if n % 2 == 0 else -1.0
    nyq = jnp.zeros((1, kf_ref.shape[1]), F32)
    for rs in chunks:
        kf, kb = kf_ref[rs, :], kb_ref[rs, :]
        kf16_ref[rs, :] = kf.astype(BF16)
        kb16_ref[rs, :] = kb.astype(BF16)
        nyq = nyq + jnp.sum(alt * kf, axis=0, keepdims=True) + sign_n * jnp.sum(alt * kb, axis=0, keepdims=True)
    kn_ref[...] = nyq / (2.0 * n)
    for fs in chunks:
        freq = fs.start + lax.broadcasted_iota(jnp.int32, (rchunk, 1), 0)
        wgt = jnp.where(freq == 0, 1.0, 2.0) / (2.0 * n)
        kp_ref[fs, :] = (dot(c_ref[fs, :], kf16_ref[...]) + alt * dot(c_ref[fs, :], kb16_ref[...])) * wgt
        kq_ref[fs, :] = (dot(s_ref[fs, :], kf16_ref[...]) + alt * dot(s_ref[fs, :], kb16_ref[...])) * wgt


def _alt_sign(r0, rows):
    row = lax.broadcasted_iota(jnp.int32, (rows, 1), 0)
    return jnp.where(row % 2 == 0, 1.0, -1.0)


def _hy_conv_kernel(x1_ref, x2_ref, v_ref, kp_ref, kq_ref, kn_ref, bias_ref, c_ref, s_ref, o_ref,
                    z_ref, z16_ref, p2_ref, q2_ref, *, n, rchunk):
    dot = functools.partial(jnp.dot, preferred_element_type=F32)
    chunks = [slice(r0, r0 + rchunk) for r0 in range(0, n, rchunk)]
    alt = _alt_sign(0, rchunk)
    gate_refs = (x1_ref, x2_ref)
    for o in range(2):
        nyq = jnp.zeros((1, z_ref.shape[1]), F32)
        for rs in chunks:
            z = v_ref[rs, :].astype(F32) if o == 0 else z_ref[rs, :]
            if o == 0:
                z_ref[rs, :] = z
            z16_ref[rs, :] = z.astype(BF16)
            nyq = nyq + jnp.sum(alt * z, axis=0, keepdims=True)
        nyq = nyq * kn_ref[o]
        for fs in chunks:
            p = dot(c_ref[fs, :], z16_ref[...])
            q = dot(s_ref[fs, :], z16_ref[...])
            kp, kq = kp_ref[o, fs, :], kq_ref[o, fs, :]
            p2_ref[fs, :] = (p * kp - q * kq).astype(BF16)
            q2_ref[fs, :] = (p * kq + q * kp).astype(BF16)
        for rs in chunks:
            zc = dot(c_ref[rs, :], p2_ref[...]) + dot(s_ref[rs, :], q2_ref[...]) + alt * nyq
            z = gate_refs[o][rs, :].astype(F32) * (zc + z_ref[rs, :] * bias_ref[pl.ds(o, 1), :])
            if o == 0:
                z_ref[rs, :] = z
            else:
                o_ref[rs, :] = z.astype(o_ref.dtype)


def _dft_tables(n):
    f = lax.broadcasted_iota(jnp.int32, (n, n), 0)
    s = lax.broadcasted_iota(jnp.int32, (n, n), 1)
    ang = ((f * s) % (2 * n)).astype(F32) * (math.pi / n)
    return jnp.cos(ang).astype(BF16), jnp.sin(ang).astype(BF16)


def _resident(shape):
    return pl.BlockSpec(shape, lambda *_: (0,) * len(shape), pipeline_mode=pl.Buffered(1))


def _hyena_spectra(n, d, f_w1, f_b1, f_w2, f_b2, f_w3, f_b3, f_freq, f_w_out, tables, tc=512):
    fw = f_w1.shape[1]
    bands = jnp.linspace(1e-4, HY_BANDS - 1, HY_BANDS, dtype=F32).reshape(1, HY_BANDS)
    f32 = lambda a: a.astype(F32)
    row = lambda a: f32(a).reshape(1, -1)
    hdn = pl.pallas_call(
        functools.partial(_hy_mlp_kernel, n=n),
        out_shape=jax.ShapeDtypeStruct((2, n, fw), F32),
        name="hy_mlp",
    )(bands, f32(f_w1[0:1]), f32(f_w1[1:1 + HY_BANDS]), f32(f_w1[1 + HY_BANDS:]), row(f_b1),
      f32(f_w2), row(f_b2), f32(f_w3), row(f_b3), f32(f_freq))
    deltas = jnp.abs(jnp.linspace(math.log(HY_DECAY_TARGET) / HY_SLOW_DECAY,
                                  math.log(HY_DECAY_TARGET) / HY_FAST_DECAY, d, dtype=F32)).reshape(1, d)
    tc = min(tc, d)
    nj = d // tc
    k = pl.pallas_call(
        functools.partial(_hy_filter_kernel, n=n),
        out_shape=jax.ShapeDtypeStruct((4, n, d), F32),
        grid=(4, nj),
        in_specs=[pl.BlockSpec((None, n, fw), lambda i, j: (i % 2, 0, 0)),
                  pl.BlockSpec((fw, tc), lambda i, j: (0, i * nj + j)),
                  pl.BlockSpec((1, tc), lambda i, j: (0, j))],
        out_specs=pl.BlockSpec((None, n, tc), lambda i, j: (i, 0, j)),
        compiler_params=_params("parallel", "parallel"),
        name="hy_filter",
    )(hdn, f32(f_w_out), deltas)
    cos_t, sin_t = tables
    tcs = min(256, d)
    kp, kq, kn = pl.pallas_call(
        functools.partial(_hy_spectrum_kernel, n=n, rchunk=min(_HY_ROWS, n)),
        out_shape=[jax.ShapeDtypeStruct((2, n, d), F32), jax.ShapeDtypeStruct((2, n, d), F32),
                   jax.ShapeDtypeStruct((2, 1, d), F32)],
        grid=(2, d // tcs),
        in_specs=[pl.BlockSpec((None, n, tcs), lambda o, j: (2 * o, 0, j)),
                  pl.BlockSpec((None, n, tcs), lambda o, j: (2 * o + 1, 0, j)),
                  _resident((n, n)), _resident((n, n))],
        out_specs=[pl.BlockSpec((None, n, tcs), lambda o, j: (o, 0, j)),
                   pl.BlockSpec((None, n, tcs), lambda o, j: (o, 0, j)),
                   pl.BlockSpec((None, 1, tcs), lambda o, j: (o, 0, j))],
        scratch_shapes=[pltpu.VMEM((n, tcs), BF16), pltpu.VMEM((n, tcs), BF16)],
        compiler_params=_params("parallel", "parallel"),
        name="hy_spectrum",
    )(k, k, cos_t, sin_t)
    return kp, kq, kn


def _hyena_conv(proj, spectra, bias, tables, tc=256):
    b, n, d3 = proj.shape
    d = d3 // 3
    tc = min(tc, d)
    nj = d // tc
    kp, kq, kn = spectra
    cos_t, sin_t = tables
    part = lambda p: pl.BlockSpec((None, n, tc), lambda j, i: (i, 0, p * nj + j))
    spec = pl.BlockSpec((2, n, tc), lambda j, i: (0, 0, j), pipeline_mode=pl.Buffered(1))
    return pl.pallas_call(
        functools.partial(_hy_conv_kernel, n=n, rchunk=min(_HY_ROWS, n)),
        out_shape=jax.ShapeDtypeStruct((b, n, d), BF16),
        grid=(nj, b),
        in_specs=[part(0), part(1), part(2), spec, spec,
                  pl.BlockSpec((2, 1, tc), lambda j, i: (0, 0, j)),
                  pl.BlockSpec((2, tc), lambda j, i: (0, j)),
                  _resident((n, n)), _resident((n, n))],
        out_specs=pl.BlockSpec((None, n, tc), lambda j, i: (i, 0, j)),
        scratch_shapes=[pltpu.VMEM((n, tc), F32), pltpu.VMEM((n, tc), BF16),
                        pltpu.VMEM((n, tc), BF16), pltpu.VMEM((n, tc), BF16)],
        compiler_params=_params("parallel", "parallel"),
        name="hy_conv",
    )(proj, proj, proj, kp, kq, kn, bias.astype(F32), cos_t, sin_t)


def _hyena_mixer(u, bsz, seq, w_in, b_in, short_w, short_b, spectra, bias, w_out, b_out, tables):
    proj = _matmul(u, w_in, bias=b_in, out_dtype=BF16).reshape(bsz, seq, -1)
    proj = _dwconv(proj, 0, short_w, short_b, lambda y: y)
    z = _hyena_conv(proj, spectra, bias, tables)
    return _matmul(z.reshape(bsz * seq, -1), w_out, bias=b_out, out_dtype=F32)


def _latent_pos_embed(n_tok, d):
    rows = n_tok // GRID_W
    r = jnp.repeat(jnp.arange(rows, dtype=F32), GRID_W)
    col = jnp.tile(jnp.arange(GRID_W, dtype=F32), rows)
    quarter = d // 4
    omega = 1.0 / (10000.0 ** (jnp.arange(quarter, dtype=F32) / quarter))
    ar = r[:, None] * omega[None]
    ac = col[:, None] * omega[None]
    return jnp.concatenate([jnp.sin(ar), jnp.cos(ar), jnp.sin(ac), jnp.cos(ac)], axis=-1)


def _pad_cols(w, n):
    return jnp.pad(w, ((0, 0), (0, n - w.shape[1])))


def kernel(x_prompt, x_sample, state_ssd, c, c_ctx, w_mod, b_mod, g_pre, g_post, ffn_w_gate, ffn_w_up, ffn_w_down, ssd_w_in, ssd_conv_w, ssd_conv_b, ssd_dt_bias, ssd_a_log, ssd_d, ssd_norm, ssd_w_out, hy_w_in, hy_b_in, hy_short_w, hy_short_b, hy_f_w1, hy_f_b1, hy_f_w2, hy_f_b2, hy_f_w3, hy_f_b3, hy_f_freq, hy_f_w_out, hy_bias, hy_w_out, hy_b_out):
    depth, d, _ = w_mod.shape
    d_ff = ffn_w_gate.shape[-1]
    d_ff_pad = -(-d_ff // _FF_ALIGN) * _FF_ALIGN
    d_inner = ssd_w_out.shape[1]
    heads = d_inner // SSD_HEAD_DIM
    n_main = ssd_w_in.shape[-1] - 2 * heads

    streams = []
    for x, rows in ((x_prompt, slice(0, 1)), (x_sample, slice(1, 1 + c.shape[0]))):
        bsz, seq, _ = x.shape
        streams.append(dict(bsz=bsz, seq=seq, x=x.reshape(bsz * seq, d), rows=rows))
    ctx, lat = streams

    cvec = jnp.concatenate([c_ctx[None], c], axis=0)
    n_cond = cvec.shape[0]
    cvec = jnp.pad(cvec, ((0, -n_cond % 8), (0, 0)))
    mod = _modulation(cvec, w_mod, b_mod).reshape(depth, -1, 3 * N_SUB, 1, d)

    def mod_vec(i, s, j, which):
        return mod[i, s["rows"], 3 * j + which]

    def pre_args(i, s, j):
        return (mod_vec(i, s, j, 0), mod_vec(i, s, j, 1), g_pre[i, j].reshape(1, d))

    def post_args(i, s, j, o, weight):
        return (o, mod_vec(i, s, j, 2), g_post[i, j].reshape(1, d), weight)

    tables = {s["seq"]: _dft_tables(s["seq"]) for s in streams} if depth > 1 else {}

    pos = _latent_pos_embed(lat["seq"], d)
    _, ctx["u"] = _post_pre(ctx["x"], ctx["seq"], pre=pre_args(0, ctx, 0))
    lat["x"], lat["u"] = _post_pre(lat["x"], lat["seq"], pos=pos, pre=pre_args(0, lat, 0))

    new_states = []
    for i in range(depth):
        kind, li = i % 2, i // 2
        for j in range(N_SUB):
            if j == 1 and kind == 0:
                w_in_main = ssd_w_in[li][:, :n_main].astype(BF16)
                w_in_dt = ssd_w_in[li][:, n_main:].astype(BF16)
                w_out = ssd_w_out[li].astype(BF16)
                outs = []
                for s, h0, emit in ((ctx, None, True), (lat, state_ssd[:, li], False)):
                    o, st = _ssd_mixer(s["u"], s["bsz"], s["seq"], h0, emit, w_in_main, w_in_dt,
                                       ssd_conv_w[li].astype(F32), ssd_conv_b[li].astype(F32),
                                       ssd_dt_bias[li], ssd_a_log[li], ssd_d[li], ssd_norm[li], w_out)
                    outs.append(o)
                    if emit:
                        new_states.append(st.astype(x_prompt.dtype))
                weight = 1.0
            elif j == 1:
                w_in, w_out = hy_w_in[li].astype(BF16), hy_w_out[li].astype(BF16)
                outs = []
                for s in streams:
                    spectra = _hyena_spectra(s["seq"], d, hy_f_w1[li], hy_f_b1[li], hy_f_w2[li], hy_f_b2[li],
                                             hy_f_w3[li], hy_f_b3[li], hy_f_freq[li], hy_f_w_out[li],
                                             tables[s["seq"]])
                    outs.append(_hyena_mixer(s["u"], s["bsz"], s["seq"], w_in, hy_b_in[li], hy_short_w[li],
                                             hy_short_b[li], spectra, hy_bias[li], w_out, hy_b_out[li],
                                             tables[s["seq"]]))
                weight = 1.0
            else:
                slot = j // 2
                w_gate = _pad_cols(ffn_w_gate[i, slot], d_ff_pad).astype(BF16)
                w_up = _pad_cols(ffn_w_up[i, slot], d_ff_pad).astype(BF16)
                w_down = jnp.pad(ffn_w_down[i, slot], ((0, d_ff_pad - d_ff), (0, 0))).astype(BF16)
                outs = [_matmul(_gate_up(s["u"], w_gate, w_up), w_down, tk=d_ff_pad // 4) for s in streams]
                weight = 0.5
            nxt = (i, j + 1) if j + 1 < N_SUB else ((i + 1, 0) if i + 1 < depth else None)
            for s, o in zip(streams, outs):
                pre = pre_args(nxt[0], s, nxt[1]) if nxt is not None else None
                s["x"], s["u"] = _post_pre(s["x"], s["seq"], post=post_args(i, s, j, o, weight), pre=pre)

    y_prompt = ctx["x"].reshape(x_prompt.shape)
    y_sample = lat["x"].reshape(x_sample.shape)
    return (y_prompt, y_sample, jnp.stack(new_states, axis=1))
```

```python
import functools
import math

import jax
import jax.numpy as jnp
from jax import lax
from jax.experimental import pallas as pl
from jax.experimental.pallas import tpu as pltpu

F32 = jnp.float32
BF16 = jnp.bfloat16
HIGHEST = lax.Precision.HIGHEST

RMS_EPS = 1e-6
N_SUB = 3
GRID_W = 64

SSD_HEAD_DIM = 64
SSD_GROUPS = 8
D_STATE = 128
SSD_CHUNK = 128

HY_BANDS = 16
HY_DECAY_TARGET = 1e-2
HY_FAST_DECAY = 0.3
HY_SLOW_DECAY = 1.5

_V7X_VMEM_BYTES = 64 * 1024 * 1024
_VMEM_LIMIT = _V7X_VMEM_BYTES - 8 * 1024 * 1024
_LANE = 128


def _params(*semantics):
    return pltpu.CompilerParams(dimension_semantics=semantics, vmem_limit_bytes=_VMEM_LIMIT)


def _tile(dim, target):
    if dim <= target:
        return dim
    for t in range(target - target % _LANE, 0, -_LANE):
        if dim % t == 0:
            return t
    raise ValueError(f"no {_LANE}-aligned tile <= {target} divides {dim}")


def _nt_dot(a, b, **kw):
    return lax.dot_general(a, b, (((1,), (1,)), ((), ())), preferred_element_type=F32, **kw)


def _silu(x):
    return x * jax.nn.sigmoid(x)


def _softplus(x):
    return jnp.maximum(x, 0.0) + jnp.log1p(jnp.exp(-jnp.abs(x)))


def _mod_kernel(c_ref, w_ref, b_ref, o_ref):
    a = _silu(c_ref[...]).astype(BF16)
    o_ref[...] = jnp.dot(a, w_ref[...].astype(BF16), preferred_element_type=F32) + b_ref[...]


def _modulation(cvec, w_mod, b_mod, tn=512):
    depth, d, n = w_mod.shape
    r = cvec.shape[0]
    return pl.pallas_call(
        _mod_kernel,
        out_shape=jax.ShapeDtypeStruct((depth, r, n), F32),
        grid=(depth, n // tn),
        in_specs=[
            pl.BlockSpec((r, d), lambda l, j: (0, 0)),
            pl.BlockSpec((None, d, tn), lambda l, j: (l, 0, j)),
            pl.BlockSpec((None, 1, tn), lambda l, j: (l, 0, j)),
        ],
        out_specs=pl.BlockSpec((None, r, tn), lambda l, j: (l, 0, j)),
        compiler_params=_params("parallel", "parallel"),
        name="modulation",
    )(cvec, w_mod, b_mod.reshape(depth, 1, n))


def _rms(v, g):
    return v * lax.rsqrt(jnp.mean(v * v, axis=-1, keepdims=True) + RMS_EPS) * g


def _post_pre_kernel(*refs, has_pos, has_post, has_pre, weight):
    refs = list(refs)
    x = refs.pop(0)[...]
    if has_pos:
        x = x + refs.pop(0)[...]
    if has_post:
        o_ref, gate_ref, gpost_ref = refs.pop(0), refs.pop(0), refs.pop(0)
        x = x + weight * gate_ref[...] * _rms(o_ref[...].astype(F32), gpost_ref[...])
    if has_pre:
        shift_ref, scale_ref, gpre_ref = refs.pop(0), refs.pop(0), refs.pop(0)
        u = _rms(x, gpre_ref[...]) * (1.0 + scale_ref[...]) + shift_ref[...]
    if has_pos or has_post:
        refs.pop(0)[...] = x
    if has_pre:
        refs.pop(0)[...] = u.astype(BF16)


def _post_pre(x, seq_len, *, pos=None, post=None, pre=None, tm=256):
    t, d = x.shape
    tm = min(tm, seq_len)
    assert seq_len % tm == 0 and t % tm == 0

    def mod_spec(m):
        if m.shape[0] == 1:
            return pl.BlockSpec((None, 1, d), lambda i: (0, 0, 0))
        return pl.BlockSpec((None, 1, d), lambda i: ((i * tm) // seq_len, 0, 0))

    row_spec = pl.BlockSpec((tm, d), lambda i: (i, 0))
    vec_spec = pl.BlockSpec((1, d), lambda i: (0, 0))
    args, in_specs = [x], [row_spec]
    if pos is not None:
        per_seq = seq_len // tm
        args.append(pos)
        in_specs.append(pl.BlockSpec((tm, d), lambda i: (i % per_seq, 0)))
    weight = 0.0
    if post is not None:
        o, gate, g_post, weight = post
        args += [o, gate, g_post]
        in_specs += [row_spec, mod_spec(gate), vec_spec]
    if pre is not None:
        shift, scale, g_pre = pre
        args += [shift, scale, g_pre]
        in_specs += [mod_spec(shift), mod_spec(scale), vec_spec]
    out_shape, out_specs = [], []
    new_x = pos is not None or post is not None
    if new_x:
        out_shape.append(jax.ShapeDtypeStruct((t, d), F32))
        out_specs.append(row_spec)
    if pre is not None:
        out_shape.append(jax.ShapeDtypeStruct((t, d), BF16))
        out_specs.append(row_spec)
    outs = pl.pallas_call(
        functools.partial(_post_pre_kernel, has_pos=pos is not None, has_post=post is not None,
                          has_pre=pre is not None, weight=weight),
        out_shape=out_shape,
        grid=(t // tm,),
        in_specs=in_specs,
        out_specs=out_specs,
        compiler_params=_params("parallel"),
        name="post_pre",
    )(*args)
    outs = list(outs)
    x_new = outs.pop(0) if new_x else None
    u = outs.pop(0) if pre is not None else None
    return x_new, u


def _mm_kernel(*refs, has_bias):
    refs = list(refs)
    x_ref, w_ref = refs.pop(0), refs.pop(0)
    b_ref = refs.pop(0) if has_bias else None
    o_ref = refs.pop(0)
    acc = jnp.dot(x_ref[...], w_ref[...], preferred_element_type=F32)
    if has_bias:
        acc = acc + b_ref[...]
    o_ref[...] = acc.astype(o_ref.dtype)


def _w_spec(w, w_index, tn, col_block0):
    lead = tuple(w_index)
    assert len(lead) == w.ndim - 2
    return pl.BlockSpec((None,) * len(lead) + (w.shape[-2], tn), lambda i, j: lead + (0, col_block0 + j))


def _matmul(x, w, *, w_index=(), col0=0, n=None, bias=None, out_dtype=F32, tm=1024, tn=1024):
    t, kdim = x.shape
    n = w.shape[-1] - col0 if n is None else n
    assert w.shape[-2] == kdim
    tm, tn = _tile(t, tm), _tile(n, tn)
    if tn % _LANE or col0 % tn:
        w, w_index, col0 = w[tuple(w_index)][:, col0:col0 + n], (), 0
    args = [x, w]
    in_specs = [pl.BlockSpec((tm, kdim), lambda i, j: (i, 0)), _w_spec(w, w_index, tn, col0 // tn)]
    if bias is not None:
        args.append(bias.reshape(1, n).astype(F32))
        in_specs.append(pl.BlockSpec((1, tn), lambda i, j: (0, j)))
    return pl.pallas_call(
        functools.partial(_mm_kernel, has_bias=bias is not None),
        out_shape=jax.ShapeDtypeStruct((t, n), out_dtype),
        grid=(t // tm, n // tn),
        in_specs=in_specs,
        out_specs=pl.BlockSpec((tm, tn), lambda i, j: (i, j)),
        compiler_params=_params("parallel", "parallel"),
        name="matmul",
    )(*args)


def _mm_conv_kernel(*refs, seq, act, has_bias):
    refs = list(refs)
    x_ref, w_ref = refs.pop(0), refs.pop(0)
    b_ref = refs.pop(0) if has_bias else None
    cw_ref, cb_ref, o_ref = refs
    acc = jnp.dot(x_ref[...], w_ref[...], preferred_element_type=F32)
    if has_bias:
        acc = acc + b_ref[...]
    tm = acc.shape[0]
    k_w = cw_ref.shape[0]
    pos = lax.rem(lax.broadcasted_iota(jnp.int32, (tm, 1), 0), seq)
    y = cb_ref[...] + acc * cw_ref[pl.ds(k_w // 2, 1), :]
    for k in range(k_w):
        d = k - k_w // 2
        if d != 0:
            shifted = pltpu.roll(acc, (-d) % tm, 0)
            inside = (pos + d >= 0) & (pos + d < seq)
            y = y + jnp.where(inside, shifted, 0.0) * cw_ref[pl.ds(k, 1), :]
    o_ref[...] = act(y).astype(o_ref.dtype)


def _matmul_conv(x, seq, w, w_index, col0, conv_w, conv_b, act, bias=None, tn=256, rows=1024):
    t, kdim = x.shape
    k_w, c = conv_w.shape
    tm = seq * max(1, min(rows, t) // seq)
    tn = _tile(c, tn)
    assert t % tm == 0 and col0 % tn == 0 and w.shape[-2] == kdim
    args = [x, w]
    in_specs = [pl.BlockSpec((tm, kdim), lambda i, j: (i, 0)), _w_spec(w, w_index, tn, col0 // tn)]
    col_spec = lambda r: pl.BlockSpec((r, tn), lambda i, j: (0, j))
    if bias is not None:
        args.append(bias.reshape(1, c).astype(F32))
        in_specs.append(col_spec(1))
    args += [conv_w.astype(F32), conv_b.reshape(1, c).astype(F32)]
    in_specs += [col_spec(k_w), col_spec(1)]
    return pl.pallas_call(
        functools.partial(_mm_conv_kernel, seq=seq, act=act, has_bias=bias is not None),
        out_shape=jax.ShapeDtypeStruct((t, c), BF16),
        grid=(t // tm, c // tn),
        in_specs=in_specs,
        out_specs=pl.BlockSpec((tm, tn), lambda i, j: (i, j)),
        compiler_params=_params("parallel", "parallel"),
        name="matmul_conv",
    )(*args)


def _gate_up_kernel(u_ref, wg_ref, wu_ref, o_ref):
    u = u_ref[...]
    g = jnp.dot(u, wg_ref[...], preferred_element_type=F32)
    v = jnp.dot(u, wu_ref[...], preferred_element_type=F32)
    o_ref[...] = (_silu(g) * v).astype(o_ref.dtype)


def _gate_up(u, w_gate, w_up, w_index, tm=2048, tn=256):
    t, d = u.shape
    n = w_gate.shape[-1]
    tm, tn = _tile(t, tm), _tile(n, tn)
    return pl.pallas_call(
        _gate_up_kernel,
        out_shape=jax.ShapeDtypeStruct((t, n), BF16),
        grid=(t // tm, n // tn),
        in_specs=[pl.BlockSpec((tm, d), lambda i, j: (i, 0)),
                  _w_spec(w_gate, w_index, tn, 0), _w_spec(w_up, w_index, tn, 0)],
        out_specs=pl.BlockSpec((tm, tn), lambda i, j: (i, j)),
        compiler_params=_params("parallel", "parallel"),
        name="gate_up",
    )(u, w_gate, w_up)


_HY_ROWS = 512


def _ssd_scan_kernel(*refs, has_h0, emit_state, hpg, p_dim):
    refs = list(refs)
    xs_ref, b_ref, c_ref, z_ref, dt_ref, dtb_ref, a_ref, dsk_ref, nw_ref = refs[:9]
    refs = refs[9:]
    h0_ref = refs.pop(0) if has_h0 else None
    y_ref = refs.pop(0)
    st_out_ref = refs.pop(0) if emit_state else None
    ybuf_ref, stf_ref, stb_ref = refs

    q = SSD_CHUNK
    seq, gp = xs_ref.shape
    nc = seq // q
    t_i = lax.broadcasted_iota(jnp.int32, (q, q), 0)
    s_i = lax.broadcasted_iota(jnp.int32, (q, q), 1)
    masks = (s_i <= t_i, s_i >= t_i)
    tris = (masks[0].astype(F32), masks[1].astype(F32))
    eye = (lax.broadcasted_iota(jnp.int32, (hpg, hpg), 0)
           == lax.broadcasted_iota(jnp.int32, (hpg, hpg), 1)).astype(F32)
    expand = (lax.broadcasted_iota(jnp.int32, (hpg, gp), 1) // p_dim
              == lax.broadcasted_iota(jnp.int32, (hpg, gp), 0)).astype(F32)
    expand_bf = expand.astype(BF16)
    pair_lane = lax.broadcasted_iota(jnp.int32, (q, 2 * p_dim), 1)

    def chunk(r0, d, st_ref):
        rows = pl.ds(r0, q)
        hs = slice(d * hpg, (d + 1) * hpg)
        xs = xs_ref[rows, :].astype(F32)
        bm, cm = b_ref[rows, :], c_ref[rows, :]
        dt = _softplus(dt_ref[rows, :][:, hs] + dtb_ref[...][:, hs])
        a = dt * a_ref[...][:, hs]
        cs = jnp.dot(tris[d], a, precision=HIGHEST, preferred_element_type=F32)
        cs_t = _nt_dot(eye, cs, precision=HIGHEST)
        end = cs[q - 1:q, :] if d == 0 else cs[0:1, :]
        dte = jnp.exp(end - cs)
        factors = jnp.concatenate([dt, dt * dte, jnp.exp(cs)], axis=0).astype(BF16)
        fx = jnp.dot(factors, expand_bf, preferred_element_type=F32)
        xg = (xs * fx[:q]).astype(BF16)
        xd = (xs * fx[q:2 * q]).astype(BF16)
        chunk_decay = jnp.dot(jnp.broadcast_to(jnp.exp(end), (8, hpg)), expand,
                              precision=HIGHEST, preferred_element_type=F32)[0:1]
        cb = _nt_dot(cm, bm)
        pieces = []
        for p in range(hpg // 2):
            xg_pair = xg[:, 2 * p * p_dim:(2 * p + 2) * p_dim]
            outs = []
            for r in (2 * p, 2 * p + 1):
                seg = cs[:, r:r + 1] - cs_t[r:r + 1, :]
                m = (cb * jnp.exp(jnp.where(masks[d], seg, -jnp.inf))).astype(BF16)
                outs.append(jnp.dot(m, xg_pair, preferred_element_type=F32))
            pieces.append(jnp.where(pair_lane < p_dim, outs[0], outs[1]))
        y_diag = jnp.concatenate(pieces, axis=1)
        st = st_ref[...]
        y_off = jnp.dot(cm, st.astype(BF16), preferred_element_type=F32) * fx[2 * q:]
        b_t = bm.astype(F32).T.astype(BF16)
        st_ref[...] = st * chunk_decay + jnp.dot(b_t, xd, preferred_element_type=F32)
        return y_diag + y_off

    if has_h0:
        stf_ref[...] = h0_ref[0]
        stb_ref[...] = h0_ref[1]
    else:
        stf_ref[...] = jnp.zeros_like(stf_ref)
        stb_ref[...] = jnp.zeros_like(stb_ref)

    def skip_body(i, carry):
        rows = pl.ds(pl.multiple_of(i * q, q), q)
        ybuf_ref[rows, :] = xs_ref[rows, :].astype(F32) * dsk_ref[...]
        return carry

    lax.fori_loop(0, nc, skip_body, 0)

    def scan_body(i, carry):
        rf = pl.multiple_of(i * q, q)
        rb = pl.multiple_of((nc - 1 - i) * q, q)
        yf = chunk(rf, 0, stf_ref)
        yb = chunk(rb, 1, stb_ref)
        ybuf_ref[pl.ds(rf, q), :] += yf
        ybuf_ref[pl.ds(rb, q), :] += yb
        return carry

    lax.fori_loop(0, nc, scan_body, 0)

    def out_body(i, carry):
        rows = pl.ds(pl.multiple_of(i * q, q), q)
        y = ybuf_ref[rows, :] * _silu(z_ref[rows, :].astype(F32))
        y_ref[rows, :] = _rms(y, nw_ref[...]).astype(y_ref.dtype)
        return carry

    lax.fori_loop(0, nc, out_body, 0)

    if emit_state:
        st_out_ref[0] = stf_ref[...]
        st_out_ref[1] = stb_ref[...]


def _ssd_scan(z, xbc, dt, dt_bias, a_head, d_skip, norm_w, h0, emit_state, d_inner):
    b, l, _ = z.shape
    g = SSD_GROUPS
    gp = d_inner // g
    n = D_STATE
    hpg = gp // SSD_HEAD_DIM
    assert gp % _LANE == 0 and l % SSD_CHUNK == 0 and n == _LANE
    xoff = d_inner // n
    in_specs = [
        pl.BlockSpec((None, l, gp), lambda i, j: (i, 0, j)),
        pl.BlockSpec((None, l, n), lambda i, j: (i, 0, xoff + j)),
        pl.BlockSpec((None, l, n), lambda i, j: (i, 0, xoff + g + j)),
        pl.BlockSpec((None, l, gp), lambda i, j: (i, 0, j)),
        pl.BlockSpec((None, None, l, 2 * hpg), lambda i, j: (i, j, 0, 0)),
        pl.BlockSpec((None, 1, 2 * hpg), lambda i, j: (j, 0, 0)),
        pl.BlockSpec((None, 1, 2 * hpg), lambda i, j: (j, 0, 0)),
        pl.BlockSpec((1, gp), lambda i, j: (0, j)),
        pl.BlockSpec((1, gp), lambda i, j: (0, j)),
    ]
    args = [xbc, xbc, xbc, z, dt, dt_bias, a_head, d_skip, norm_w]
    state_spec = pl.BlockSpec((None, 2, None, n, gp), lambda i, j: (i, 0, j, 0, 0))
    if h0 is not None:
        in_specs.append(state_spec)
        args.append(h0)
    out_shape = [jax.ShapeDtypeStruct((b, l, d_inner), BF16)]
    out_specs = [pl.BlockSpec((None, l, gp), lambda i, j: (i, 0, j))]
    if emit_state:
        out_shape.append(jax.ShapeDtypeStruct((b, 2, g, n, gp), F32))
        out_specs.append(state_spec)
    outs = pl.pallas_call(
        functools.partial(_ssd_scan_kernel, has_h0=h0 is not None, emit_state=emit_state,
                          hpg=hpg, p_dim=SSD_HEAD_DIM),
        out_shape=out_shape,
        grid=(b, g),
        in_specs=in_specs,
        out_specs=out_specs,
        scratch_shapes=[pltpu.VMEM((l, gp), F32), pltpu.VMEM((n, gp), F32), pltpu.VMEM((n, gp), F32)],
        compiler_params=_params("parallel", "parallel"),
        name="ssd_scan",
    )(*args)
    return outs[0], (outs[1] if emit_state else None)


def _ssd_mixer(u, bsz, seq, h0, emit_state, w_in, w_out, li, conv_w, conv_b, dt_bias, a_log, d_skip, norm_w):
    d_inner = w_out.shape[1]
    heads = d_inner // SSD_HEAD_DIM
    g = SSD_GROUPS
    hpg = heads // g
    gp = d_inner // g
    n_main = w_in.shape[-1] - 2 * heads
    z = _matmul(u, w_in, w_index=(li,), n=d_inner, out_dtype=BF16).reshape(bsz, seq, d_inner)
    xbc = _matmul_conv(u, seq, w_in, (li,), d_inner, conv_w, conv_b, _silu).reshape(bsz, seq, n_main - d_inner)
    dt = _matmul(u, w_in, w_index=(li,), col0=n_main, out_dtype=F32, tn=2 * heads)
    dt = dt.reshape(bsz, seq, 2, g, hpg).transpose(0, 3, 1, 2, 4).reshape(bsz, g, seq, 2 * hpg)
    per_group = lambda v: v.astype(F32).reshape(2, g, hpg).transpose(1, 0, 2).reshape(g, 1, 2 * hpg)
    if h0 is not None:
        h0 = h0.astype(F32).reshape(bsz, 2, g, hpg, SSD_HEAD_DIM, D_STATE)
        h0 = h0.transpose(0, 1, 2, 5, 3, 4).reshape(bsz, 2, g, D_STATE, gp)
    y, st = _ssd_scan(z, xbc, dt, per_group(dt_bias), per_group(-jnp.exp(a_log.astype(F32))),
                      jnp.repeat(d_skip.astype(F32), SSD_HEAD_DIM).reshape(1, d_inner),
                      norm_w.astype(F32).reshape(1, d_inner), h0, emit_state, d_inner)
    out = _matmul(y.reshape(bsz * seq, d_inner), w_out, w_index=(li,), out_dtype=BF16, tm=512, tn=512)
    if st is not None:
        st = st.reshape(bsz, 2, g, D_STATE, hpg, SSD_HEAD_DIM).transpose(0, 1, 2, 4, 5, 3)
        st = st.reshape(bsz, 2, heads, SSD_HEAD_DIM, D_STATE)
    return out, st


def _hy_mlp_kernel(bands_ref, w1t_ref, w1c_ref, w1s_ref, b1_ref, w2_ref, b2_ref, w3_ref, b3_ref, fq_ref,
                   o_ref, *, n):
    idx = lax.broadcasted_iota(jnp.int32, (n, 1), 0).astype(F32)
    hdot = functools.partial(jnp.dot, precision=HIGHEST, preferred_element_type=F32)
    for direction in range(2):
        pos = idx if direction == 0 else (n - 1.0) - idx
        t = pos / (n - 1.0)
        ang = (2.0 * math.pi * pos / n) * bands_ref[...]
        h = t * w1t_ref[...] + hdot(jnp.cos(ang), w1c_ref[...]) - hdot(jnp.sin(ang), w1s_ref[...])
        h = jnp.sin(fq_ref[0:1, :] * (h + b1_ref[...]))
        h = jnp.sin(fq_ref[1:2, :] * (hdot(h, w2_ref[...]) + b2_ref[...]))
        h = jnp.sin(fq_ref[2:3, :] * (hdot(h, w3_ref[...]) + b3_ref[...]))
        o_ref[direction] = h


def _hy_filter_kernel(h_ref, w_ref, delta_ref, o_ref, *, n):
    direction = pl.program_id(0) % 2
    idx = lax.broadcasted_iota(jnp.int32, (n, 1), 0).astype(F32)
    pos = jnp.where(direction == 0, idx, (n - 1.0) - idx)
    t = pos / (n - 1.0)
    k = jnp.dot(h_ref[...], w_ref[...], precision=HIGHEST, preferred_element_type=F32)
    k = k * jnp.exp(-t * delta_ref[...])
    o_ref[...] = k / (jnp.sum(jnp.abs(k), axis=0, keepdims=True) + RMS_EPS)


def _hy_spectrum_kernel(kf_ref, kb_ref, c_ref, s_ref, kp_ref, kq_ref, kn_ref, kf16_ref, kb16_ref, *, n, rchunk):
    dot = functools.partial(jnp.dot, preferred_element_type=F32)
    chunks = [slice(r0, r0 + rchunk) for r0 in range(0, n, rchunk)]
    alt = _alt_sign(0, rchunk)
    sign_n = 1.0 if n % 2 == 0 else -1.0
    nyq = jnp.zeros((1, kf_ref.shape[1]), F32)
    for rs in chunks:
        kf, kb = kf_ref[rs, :], kb_ref[rs, :]
        kf16_ref[rs, :] = kf.astype(BF16)
        kb16_ref[rs, :] = kb.astype(BF16)
        nyq = nyq + jnp.sum(alt * kf, axis=0, keepdims=True) + sign_n * jnp.sum(alt * kb, axis=0, keepdims=True)
    kn_ref[...] = nyq / (2.0 * n)
    for fs in chunks:
        freq = fs.start + lax.broadcasted_iota(jnp.int32, (rchunk, 1), 0)
        wgt = jnp.where(freq == 0, 1.0, 2.0) / (2.0 * n)
        kp_ref[fs, :] = (dot(c_ref[fs, :], kf16_ref[...]) + alt * dot(c_ref[fs, :], kb16_ref[...])) * wgt
        kq_ref[fs, :] = (dot(s_ref[fs, :], kf16_ref[...]) + alt * dot(s_ref[fs, :], kb16_ref[...])) * wgt


def _alt_sign(r0, rows):
    row = lax.broadcasted_iota(jnp.int32, (rows, 1), 0)
    return jnp.where(row % 2 == 0, 1.0, -1.0)


def _hy_conv_kernel(x1_ref, x2_ref, v_ref, kp_ref, kq_ref, kn_ref, bias_ref, c_ref, s_ref, o_ref,
                    z_ref, z16_ref, p2_ref, q2_ref, *, n, rchunk):
    dot = functools.partial(jnp.dot, preferred_element_type=F32)
    chunks = [slice(r0, r0 + rchunk) for r0 in range(0, n, rchunk)]
    alt = _alt_sign(0, rchunk)
    gate_refs = (x1_ref, x2_ref)
    for o in range(2):
        nyq = jnp.zeros((1, z_ref.shape[1]), F32)
        for rs in chunks:
            z = v_ref[rs, :].astype(F32) if o == 0 else z_ref[rs, :]
            if o == 0:
                z_ref[rs, :] = z
            z16_ref[rs, :] = z.astype(BF16)
            nyq = nyq + jnp.sum(alt * z, axis=0, keepdims=True)
        nyq = nyq * kn_ref[o]
        for fs in chunks:
            p = dot(c_ref[fs, :], z16_ref[...])
            q = dot(s_ref[fs, :], z16_ref[...])
            kp, kq = kp_ref[o, fs, :], kq_ref[o, fs, :]
            p2_ref[fs, :] = (p * kp - q * kq).astype(BF16)
            q2_ref[fs, :] = (p * kq + q * kp).astype(BF16)
        for rs in chunks:
            zc = dot(c_ref[rs, :], p2_ref[...]) + dot(s_ref[rs, :], q2_ref[...]) + alt * nyq
            z = gate_refs[o][rs, :].astype(F32) * (zc + z_ref[rs, :] * bias_ref[pl.ds(o, 1), :])
            if o == 0:
                z_ref[rs, :] = z
            else:
                o_ref[rs, :] = z.astype(o_ref.dtype)


def _dft_tables(n):
    f = lax.broadcasted_iota(jnp.int32, (n, n), 0)
    s = lax.broadcasted_iota(jnp.int32, (n, n), 1)
    ang = ((f * s) % (2 * n)).astype(F32) * (math.pi / n)
    return jnp.cos(ang).astype(BF16), jnp.sin(ang).astype(BF16)


def _resident(shape):
    return pl.BlockSpec(shape, lambda *_: (0,) * len(shape), pipeline_mode=pl.Buffered(1))


def _hyena_spectra(n, d, f_w1, f_b1, f_w2, f_b2, f_w3, f_b3, f_freq, f_w_out, tables, tc=512):
    fw = f_w1.shape[1]
    bands = jnp.linspace(1e-4, HY_BANDS - 1, HY_BANDS, dtype=F32).reshape(1, HY_BANDS)
    f32 = lambda a: a.astype(F32)
    row = lambda a: f32(a).reshape(1, -1)
    hdn = pl.pallas_call(
        functools.partial(_hy_mlp_kernel, n=n),
        out_shape=jax.ShapeDtypeStruct((2, n, fw), F32),
        name="hy_mlp",
    )(bands, f32(f_w1[0:1]), f32(f_w1[1:1 + HY_BANDS]), f32(f_w1[1 + HY_BANDS:]), row(f_b1),
      f32(f_w2), row(f_b2), f32(f_w3), row(f_b3), f32(f_freq))
    deltas = jnp.abs(jnp.linspace(math.log(HY_DECAY_TARGET) / HY_SLOW_DECAY,
                                  math.log(HY_DECAY_TARGET) / HY_FAST_DECAY, d, dtype=F32)).reshape(1, d)
    tc = min(tc, d)
    nj = d // tc
    k = pl.pallas_call(
        functools.partial(_hy_filter_kernel, n=n),
        out_shape=jax.ShapeDtypeStruct((4, n, d), F32),
        grid=(4, nj),
        in_specs=[pl.BlockSpec((None, n, fw), lambda i, j: (i % 2, 0, 0)),
                  pl.BlockSpec((fw, tc), lambda i, j: (0, i * nj + j)),
                  pl.BlockSpec((1, tc), lambda i, j: (0, j))],
        out_specs=pl.BlockSpec((None, n, tc), lambda i, j: (i, 0, j)),
        compiler_params=_params("parallel", "parallel"),
        name="hy_filter",
    )(hdn, f32(f_w_out), deltas)
    cos_t, sin_t = tables
    tcs = min(256, d)
    kp, kq, kn = pl.pallas_call(
        functools.partial(_hy_spectrum_kernel, n=n, rchunk=min(_HY_ROWS, n)),
        out_shape=[jax.ShapeDtypeStruct((2, n, d), F32), jax.ShapeDtypeStruct((2, n, d), F32),
                   jax.ShapeDtypeStruct((2, 1, d), F32)],
        grid=(2, d // tcs),
        in_specs=[pl.BlockSpec((None, n, tcs), lambda o, j: (2 * o, 0, j)),
                  pl.BlockSpec((None, n, tcs), lambda o, j: (2 * o + 1, 0, j)),
                  _resident((n, n)), _resident((n, n))],
        out_specs=[pl.BlockSpec((None, n, tcs), lambda o, j: (o, 0, j)),
                   pl.BlockSpec((None, n, tcs), lambda o, j: (o, 0, j)),
                   pl.BlockSpec((None, 1, tcs), lambda o, j: (o, 0, j))],
        scratch_shapes=[pltpu.VMEM((n, tcs), BF16), pltpu.VMEM((n, tcs), BF16)],
        compiler_params=_params("parallel", "parallel"),
        name="hy_spectrum",
    )(k, k, cos_t, sin_t)
    return kp, kq, kn


def _hyena_conv(proj, spectra, bias, tables, tc=256):
    b, n, d3 = proj.shape
    d = d3 // 3
    tc = min(tc, d)
    nj = d // tc
    kp, kq, kn = spectra
    cos_t, sin_t = tables
    part = lambda p: pl.BlockSpec((None, n, tc), lambda j, i: (i, 0, p * nj + j))
    spec = pl.BlockSpec((2, n, tc), lambda j, i: (0, 0, j), pipeline_mode=pl.Buffered(1))
    return pl.pallas_call(
        functools.partial(_hy_conv_kernel, n=n, rchunk=min(_HY_ROWS, n)),
        out_shape=jax.ShapeDtypeStruct((b, n, d), BF16),
        grid=(nj, b),
        in_specs=[part(0), part(1), part(2), spec, spec,
                  pl.BlockSpec((2, 1, tc), lambda j, i: (0, 0, j)),
                  pl.BlockSpec((2, tc), lambda j, i: (0, j)),
                  _resident((n, n)), _resident((n, n))],
        out_specs=pl.BlockSpec((None, n, tc), lambda j, i: (i, 0, j)),
        scratch_shapes=[pltpu.VMEM((n, tc), F32), pltpu.VMEM((n, tc), BF16),
                        pltpu.VMEM((n, tc), BF16), pltpu.VMEM((n, tc), BF16)],
        compiler_params=_params("parallel", "parallel"),
        name="hy_conv",
    )(proj, proj, proj, kp, kq, kn, bias.astype(F32), cos_t, sin_t)


def _hyena_mixer(u, bsz, seq, w_in, w_out, li, b_in, short_w, short_b, spectra, bias, b_out, tables):
    proj = _matmul_conv(u, seq, w_in, (li,), 0, short_w, short_b, lambda y: y, bias=b_in)
    z = _hyena_conv(proj.reshape(bsz, seq, -1), spectra, bias, tables)
    return _matmul(z.reshape(bsz * seq, -1), w_out, w_index=(li,), bias=b_out, out_dtype=BF16)


def _latent_pos_embed(n_tok, d):
    rows = n_tok // GRID_W
    r = jnp.repeat(jnp.arange(rows, dtype=F32), GRID_W)
    col = jnp.tile(jnp.arange(GRID_W, dtype=F32), rows)
    quarter = d // 4
    omega = 1.0 / (10000.0 ** (jnp.arange(quarter, dtype=F32) / quarter))
    ar = r[:, None] * omega[None]
    ac = col[:, None] * omega[None]
    return jnp.concatenate([jnp.sin(ar), jnp.cos(ar), jnp.sin(ac), jnp.cos(ac)], axis=-1)


def kernel(x_prompt, x_sample, state_ssd, c, c_ctx, w_mod, b_mod, g_pre, g_post, ffn_w_gate, ffn_w_up, ffn_w_down, ssd_w_in, ssd_conv_w, ssd_conv_b, ssd_dt_bias, ssd_a_log, ssd_d, ssd_norm, ssd_w_out, hy_w_in, hy_b_in, hy_short_w, hy_short_b, hy_f_w1, hy_f_b1, hy_f_w2, hy_f_b2, hy_f_w3, hy_f_b3, hy_f_freq, hy_f_w_out, hy_bias, hy_w_out, hy_b_out):
    depth, d, _ = w_mod.shape
    ffn_w_gate16, ffn_w_up16, ffn_w_down16 = (w.astype(BF16) for w in (ffn_w_gate, ffn_w_up, ffn_w_down))
    ssd_w_in16, ssd_w_out16 = ssd_w_in.astype(BF16), ssd_w_out.astype(BF16)
    hy_w_in16, hy_w_out16 = hy_w_in.astype(BF16), hy_w_out.astype(BF16)

    streams = []
    for x, rows in ((x_prompt, slice(0, 1)), (x_sample, slice(1, 1 + c.shape[0]))):
        bsz, seq, _ = x.shape
        streams.append(dict(bsz=bsz, seq=seq, x=x.reshape(bsz * seq, d), rows=rows))
    ctx, lat = streams

    cvec = jnp.concatenate([c_ctx[None], c], axis=0)
    n_cond = cvec.shape[0]
    cvec = jnp.pad(cvec, ((0, -n_cond % 8), (0, 0)))
    mod = _modulation(cvec, w_mod, b_mod).reshape(depth, -1, 3 * N_SUB, 1, d)

    def mod_vec(i, s, j, which):
        return mod[i, s["rows"], 3 * j + which]

    def pre_args(i, s, j):
        return (mod_vec(i, s, j, 0), mod_vec(i, s, j, 1), g_pre[i, j].reshape(1, d))

    def post_args(i, s, j, o, weight):
        return (o, mod_vec(i, s, j, 2), g_post[i, j].reshape(1, d), weight)

    tables = {s["seq"]: _dft_tables(s["seq"]) for s in streams} if depth > 1 else {}

    pos = _latent_pos_embed(lat["seq"], d)
    _, ctx["u"] = _post_pre(ctx["x"], ctx["seq"], pre=pre_args(0, ctx, 0))
    lat["x"], lat["u"] = _post_pre(lat["x"], lat["seq"], pos=pos, pre=pre_args(0, lat, 0))

    new_states = []
    for i in range(depth):
        kind, li = i % 2, i // 2
        for j in range(N_SUB):
            if j == 1 and kind == 0:
                outs = []
                for s, h0, emit in ((ctx, None, True), (lat, state_ssd[:, li], False)):
                    o, st = _ssd_mixer(s["u"], s["bsz"], s["seq"], h0, emit, ssd_w_in16, ssd_w_out16, li,
                                       ssd_conv_w[li], ssd_conv_b[li], ssd_dt_bias[li], ssd_a_log[li],
                                       ssd_d[li], ssd_norm[li])
                    outs.append(o)
                    if emit:
                        new_states.append(st.astype(x_prompt.dtype))
                weight = 1.0
            elif j == 1:
                outs = []
                for s in streams:
                    spectra = _hyena_spectra(s["seq"], d, hy_f_w1[li], hy_f_b1[li], hy_f_w2[li], hy_f_b2[li],
                                             hy_f_w3[li], hy_f_b3[li], hy_f_freq[li], hy_f_w_out[li],
                                             tables[s["seq"]])
                    outs.append(_hyena_mixer(s["u"], s["bsz"], s["seq"], hy_w_in16, hy_w_out16, li, hy_b_in[li],
                                             hy_short_w[li], hy_short_b[li], spectra, hy_bias[li], hy_b_out[li],
                                             tables[s["seq"]]))
                weight = 1.0
            else:
                slot = (i, j // 2)
                outs = [_matmul(_gate_up(s["u"], ffn_w_gate16, ffn_w_up16, slot), ffn_w_down16, w_index=slot,
                                out_dtype=BF16, tm=512, tn=512) for s in streams]
                weight = 0.5
            nxt = (i, j + 1) if j + 1 < N_SUB else ((i + 1, 0) if i + 1 < depth else None)
            for s, o in zip(streams, outs):
                pre = pre_args(nxt[0], s, nxt[1]) if nxt is not None else None
                s["x"], s["u"] = _post_pre(s["x"], s["seq"], post=post_args(i, s, j, o, weight), pre=pre)

    y_prompt = ctx["x"].reshape(x_prompt.shape)
    y_sample = lat["x"].reshape(x_sample.shape)
    return (y_prompt, y_sample, jnp.stack(new_states, axis=1))
```

```python
import functools
import math

import jax
import jax.numpy as jnp
from jax import lax
from jax.experimental import pallas as pl
from jax.experimental.pallas import tpu as pltpu

F32 = jnp.float32
BF16 = jnp.bfloat16
HIGHEST = lax.Precision.HIGHEST

RMS_EPS = 1e-6
N_SUB = 3
GRID_W = 64

SSD_HEAD_DIM = 64
SSD_GROUPS = 8
D_STATE = 128
SSD_CHUNK = 128

HY_BANDS = 16
HY_DECAY_TARGET = 1e-2
HY_FAST_DECAY = 0.3
HY_SLOW_DECAY = 1.5

_V7X_VMEM_BYTES = 64 * 1024 * 1024
_VMEM_LIMIT = _V7X_VMEM_BYTES - 8 * 1024 * 1024
_LANE = 128


def _params(*semantics):
    return pltpu.CompilerParams(dimension_semantics=semantics, vmem_limit_bytes=_VMEM_LIMIT)


def _tile(dim, target):
    if dim <= target:
        return dim
    for t in range(target - target % _LANE, 0, -_LANE):
        if dim % t == 0:
            return t
    raise ValueError(f"no {_LANE}-aligned tile <= {target} divides {dim}")


def _nt_dot(a, b, **kw):
    return lax.dot_general(a, b, (((1,), (1,)), ((), ())), preferred_element_type=F32, **kw)


def _silu(x):
    return x * jax.nn.sigmoid(x)


def _softplus(x):
    return jnp.maximum(x, 0.0) + jnp.log1p(jnp.exp(-jnp.abs(x)))


def _mod_kernel(c_ref, w_ref, b_ref, o_ref):
    a = _silu(c_ref[...]).astype(BF16)
    o_ref[...] = jnp.dot(a, w_ref[...].astype(BF16), preferred_element_type=F32) + b_ref[...]


def _modulation(cvec, w_mod, b_mod, tn=512):
    depth, d, n = w_mod.shape
    r = cvec.shape[0]
    return pl.pallas_call(
        _mod_kernel,
        out_shape=jax.ShapeDtypeStruct((depth, r, n), F32),
        grid=(depth, n // tn),
        in_specs=[
            pl.BlockSpec((r, d), lambda l, j: (0, 0)),
            pl.BlockSpec((None, d, tn), lambda l, j: (l, 0, j)),
            pl.BlockSpec((None, 1, tn), lambda l, j: (l, 0, j)),
        ],
        out_specs=pl.BlockSpec((None, r, tn), lambda l, j: (l, 0, j)),
        compiler_params=_params("parallel", "parallel"),
        name="modulation",
    )(cvec, w_mod, b_mod.reshape(depth, 1, n))


def _rms(v, g):
    return v * lax.rsqrt(jnp.mean(v * v, axis=-1, keepdims=True) + RMS_EPS) * g


def _post_pre_kernel(*refs, has_pos, has_post, has_pre, weight):
    refs = list(refs)
    x = refs.pop(0)[...]
    if has_pos:
        x = x + refs.pop(0)[...]
    if has_post:
        o_ref, gate_ref, gpost_ref = refs.pop(0), refs.pop(0), refs.pop(0)
        x = x + weight * gate_ref[...] * _rms(o_ref[...].astype(F32), gpost_ref[...])
    if has_pre:
        shift_ref, scale_ref, gpre_ref = refs.pop(0), refs.pop(0), refs.pop(0)
        u = _rms(x, gpre_ref[...]) * (1.0 + scale_ref[...]) + shift_ref[...]
    if has_pos or has_post:
        refs.pop(0)[...] = x
    if has_pre:
        refs.pop(0)[...] = u.astype(BF16)


def _post_pre(x, seq_len, *, pos=None, post=None, pre=None, tm=256):
    t, d = x.shape
    tm = min(tm, seq_len)
    assert seq_len % tm == 0 and t % tm == 0

    def mod_spec(m):
        if m.shape[0] == 1:
            return pl.BlockSpec((None, 1, d), lambda i: (0, 0, 0))
        return pl.BlockSpec((None, 1, d), lambda i: ((i * tm) // seq_len, 0, 0))

    row_spec = pl.BlockSpec((tm, d), lambda i: (i, 0))
    vec_spec = pl.BlockSpec((1, d), lambda i: (0, 0))
    args, in_specs = [x], [row_spec]
    if pos is not None:
        per_seq = seq_len // tm
        args.append(pos)
        in_specs.append(pl.BlockSpec((tm, d), lambda i: (i % per_seq, 0)))
    weight = 0.0
    if post is not None:
        o, gate, g_post, weight = post
        args += [o, gate, g_post]
        in_specs += [row_spec, mod_spec(gate), vec_spec]
    if pre is not None:
        shift, scale, g_pre = pre
        args += [shift, scale, g_pre]
        in_specs += [mod_spec(shift), mod_spec(scale), vec_spec]
    out_shape, out_specs = [], []
    new_x = pos is not None or post is not None
    if new_x:
        out_shape.append(jax.ShapeDtypeStruct((t, d), F32))
        out_specs.append(row_spec)
    if pre is not None:
        out_shape.append(jax.ShapeDtypeStruct((t, d), BF16))
        out_specs.append(row_spec)
    outs = pl.pallas_call(
        functools.partial(_post_pre_kernel, has_pos=pos is not None, has_post=post is not None,
                          has_pre=pre is not None, weight=weight),
        out_shape=out_shape,
        grid=(t // tm,),
        in_specs=in_specs,
        out_specs=out_specs,
        compiler_params=_params("parallel"),
        name="post_pre",
    )(*args)
    outs = list(outs)
    x_new = outs.pop(0) if new_x else None
    u = outs.pop(0) if pre is not None else None
    return x_new, u


def _mm_kernel(*refs, has_bias):
    refs = list(refs)
    x_ref, w_ref = refs.pop(0), refs.pop(0)
    b_ref = refs.pop(0) if has_bias else None
    o_ref = refs.pop(0)
    acc = jnp.dot(x_ref[...], w_ref[...].astype(BF16), preferred_element_type=F32)
    if has_bias:
        acc = acc + b_ref[...]
    o_ref[...] = acc.astype(o_ref.dtype)


def _row_tile_spec(tm, kdim):
    return pl.BlockSpec((tm, kdim), lambda i, j: (i, 0), pipeline_mode=pl.Buffered(1))


def _w_spec(w, w_index, tn, col_block0):
    lead = tuple(w_index)
    assert len(lead) == w.ndim - 2
    return pl.BlockSpec((None,) * len(lead) + (w.shape[-2], tn), lambda i, j: lead + (0, col_block0 + j))


def _matmul(x, w, *, w_index=(), col0=0, n=None, bias=None, out_dtype=F32, tm=1024, tn=1024):
    t, kdim = x.shape
    n = w.shape[-1] - col0 if n is None else n
    assert w.shape[-2] == kdim
    tm, tn = _tile(t, tm), _tile(n, tn)
    if tn % _LANE or col0 % tn:
        w, w_index, col0 = w[tuple(w_index)][:, col0:col0 + n], (), 0
    args = [x, w]
    in_specs = [pl.BlockSpec((tm, kdim), lambda i, j: (i, 0)), _w_spec(w, w_index, tn, col0 // tn)]
    if bias is not None:
        args.append(bias.reshape(1, n).astype(F32))
        in_specs.append(pl.BlockSpec((1, tn), lambda i, j: (0, j)))
    return pl.pallas_call(
        functools.partial(_mm_kernel, has_bias=bias is not None),
        out_shape=jax.ShapeDtypeStruct((t, n), out_dtype),
        grid=(t // tm, n // tn),
        in_specs=in_specs,
        out_specs=pl.BlockSpec((tm, tn), lambda i, j: (i, j)),
        compiler_params=_params("parallel", "parallel"),
        name="matmul",
    )(*args)


def _mm_conv_kernel(*refs, seq, act, has_bias):
    refs = list(refs)
    x_ref, w_ref = refs.pop(0), refs.pop(0)
    b_ref = refs.pop(0) if has_bias else None
    cw_ref, cb_ref, o_ref = refs
    acc = jnp.dot(x_ref[...], w_ref[...].astype(BF16), preferred_element_type=F32)
    if has_bias:
        acc = acc + b_ref[...]
    tm = acc.shape[0]
    k_w = cw_ref.shape[0]
    pos = lax.rem(lax.broadcasted_iota(jnp.int32, (tm, 1), 0), seq)
    y = cb_ref[...] + acc * cw_ref[pl.ds(k_w // 2, 1), :]
    for k in range(k_w):
        d = k - k_w // 2
        if d != 0:
            shifted = pltpu.roll(acc, (-d) % tm, 0)
            inside = (pos + d >= 0) & (pos + d < seq)
            y = y + jnp.where(inside, shifted, 0.0) * cw_ref[pl.ds(k, 1), :]
    o_ref[...] = act(y).astype(o_ref.dtype)


def _matmul_conv(x, seq, w, w_index, col0, conv_w, conv_b, act, bias=None, tn=256, rows=1024):
    t, kdim = x.shape
    k_w, c = conv_w.shape
    tm = seq * max(1, min(rows, t) // seq)
    tn = _tile(c, tn)
    assert t % tm == 0 and col0 % tn == 0 and w.shape[-2] == kdim
    args = [x, w]
    in_specs = [_row_tile_spec(tm, kdim), _w_spec(w, w_index, tn, col0 // tn)]
    col_spec = lambda r: pl.BlockSpec((r, tn), lambda i, j: (0, j))
    if bias is not None:
        args.append(bias.reshape(1, c).astype(F32))
        in_specs.append(col_spec(1))
    args += [conv_w.astype(F32), conv_b.reshape(1, c).astype(F32)]
    in_specs += [col_spec(k_w), col_spec(1)]
    return pl.pallas_call(
        functools.partial(_mm_conv_kernel, seq=seq, act=act, has_bias=bias is not None),
        out_shape=jax.ShapeDtypeStruct((t, c), BF16),
        grid=(t // tm, c // tn),
        in_specs=in_specs,
        out_specs=pl.BlockSpec((tm, tn), lambda i, j: (i, j)),
        compiler_params=_params("parallel", "parallel"),
        name="matmul_conv",
    )(*args)


def _gate_up_kernel(u_ref, wg_ref, wu_ref, o_ref):
    u = u_ref[...]
    g = jnp.dot(u, wg_ref[...].astype(BF16), preferred_element_type=F32)
    v = jnp.dot(u, wu_ref[...].astype(BF16), preferred_element_type=F32)
    o_ref[...] = (_silu(g) * v).astype(o_ref.dtype)


def _gate_up(u, w_gate, w_up, w_index, tm=2048, tn=256):
    t, d = u.shape
    n = w_gate.shape[-1]
    tm, tn = _tile(t, tm), _tile(n, tn)
    return pl.pallas_call(
        _gate_up_kernel,
        out_shape=jax.ShapeDtypeStruct((t, n), BF16),
        grid=(t // tm, n // tn),
        in_specs=[_row_tile_spec(tm, d), _w_spec(w_gate, w_index, tn, 0), _w_spec(w_up, w_index, tn, 0)],
        out_specs=pl.BlockSpec((tm, tn), lambda i, j: (i, j)),
        compiler_params=_params("parallel", "parallel"),
        name="gate_up",
    )(u, w_gate, w_up)


_HY_ROWS = 512


def _ssd_scan_kernel(*refs, has_h0, emit_state, hpg, p_dim):
    refs = list(refs)
    xs_ref, b_ref, c_ref, z_ref, dt_ref, dtb_ref, a_ref, dsk_ref, nw_ref = refs[:9]
    refs = refs[9:]
    h0_ref = refs.pop(0) if has_h0 else None
    y_ref = refs.pop(0)
    st_out_ref = refs.pop(0) if emit_state else None
    ybuf_ref, stf_ref, stb_ref, cs_ref, cst_ref, fac_ref = refs

    q = SSD_CHUNK
    seq, gp = xs_ref.shape
    nc = seq // q
    table_unroll = math.gcd(nc, 4)
    t_i = lax.broadcasted_iota(jnp.int32, (q, q), 0)
    s_i = lax.broadcasted_iota(jnp.int32, (q, q), 1)
    masks = (s_i <= t_i, s_i >= t_i)
    tris = (masks[0].astype(F32), masks[1].astype(F32))
    eye = (lax.broadcasted_iota(jnp.int32, (hpg, hpg), 0)
           == lax.broadcasted_iota(jnp.int32, (hpg, hpg), 1)).astype(F32)
    expand_bf = (lax.broadcasted_iota(jnp.int32, (hpg, gp), 1) // p_dim
                 == lax.broadcasted_iota(jnp.int32, (hpg, gp), 0)).astype(F32).astype(BF16)
    pair_lane = lax.broadcasted_iota(jnp.int32, (q, 2 * p_dim), 1)

    st_refs = (stf_ref, stb_ref)
    both = (0, 1)

    def chunk_rows(c):
        return pl.ds(pl.multiple_of(c * q, q), q)

    def decay_tables(chunks):
        jobs = [(c, d) for c in chunks for d in both]
        hs = [slice(d * hpg, (d + 1) * hpg) for d in both]
        dt = [_softplus(dt_ref[chunk_rows(c), :][:, hs[d]] + dtb_ref[...][:, hs[d]]) for c, d in jobs]
        a = [dt[k] * a_ref[...][:, hs[d]] for k, (c, d) in enumerate(jobs)]
        cs = [jnp.dot(tris[d], a[k], precision=HIGHEST, preferred_element_type=F32) for k, (c, d) in enumerate(jobs)]
        cs_t = [_nt_dot(eye, cs[k], precision=HIGHEST) for k in range(len(jobs))]
        end = [cs[k][q - 1:q, :] if d == 0 else cs[k][0:1, :] for k, (c, d) in enumerate(jobs)]
        for k, (c, d) in enumerate(jobs):
            cs_ref[d, c] = cs[k]
            cst_ref[d, c] = cs_t[k]
            dec = jnp.broadcast_to(jnp.exp(end[k]), (8, hpg))
            dec_hi = dec.astype(BF16)
            dec_lo = (dec - dec_hi.astype(F32)).astype(BF16)
            rows3 = jnp.concatenate([dt[k], dt[k] * jnp.exp(end[k] - cs[k]), jnp.exp(cs[k])], axis=0).astype(BF16)
            fac_ref[d, c] = jnp.concatenate([rows3, dec_hi, dec_lo], axis=0)

    def chunk_pair(cidx):
        rows = [chunk_rows(c) for c in cidx]
        cs = [cs_ref[d, cidx[d]] for d in both]
        cs_t = [cst_ref[d, cidx[d]] for d in both]
        fx = [jnp.dot(fac_ref[d, cidx[d]], expand_bf, preferred_element_type=F32) for d in both]
        chunk_decay = [fx[d][3 * q:3 * q + 1] + fx[d][3 * q + 8:3 * q + 9] for d in both]
        bm = [b_ref[rows[d], :] for d in both]
        cm = [c_ref[rows[d], :] for d in both]
        cb = [_nt_dot(cm[d], bm[d]) for d in both]
        xs = [xs_ref[rows[d], :].astype(F32) for d in both]
        xg = [(xs[d] * fx[d][:q]).astype(BF16) for d in both]
        xd = [(xs[d] * fx[d][q:2 * q]).astype(BF16) for d in both]
        st = [st_refs[d][...] for d in both]
        y_off = [jnp.dot(cm[d], st[d].astype(BF16), preferred_element_type=F32) * fx[d][2 * q:3 * q] for d in both]
        b_t = [bm[d].astype(F32).T.astype(BF16) for d in both]
        for d in both:
            st_refs[d][...] = st[d] * chunk_decay[d] + jnp.dot(b_t[d], xd[d], preferred_element_type=F32)
        pieces = ([], [])
        for p in range(hpg // 2):
            for d in both:
                xg_pair = xg[d][:, 2 * p * p_dim:(2 * p + 2) * p_dim]
                outs = []
                for r in (2 * p, 2 * p + 1):
                    seg = cs[d][:, r:r + 1] - cs_t[d][r:r + 1, :]
                    m = (cb[d] * jnp.exp(jnp.where(masks[d], seg, -jnp.inf))).astype(BF16)
                    outs.append(jnp.dot(m, xg_pair, preferred_element_type=F32))
                pieces[d].append(jnp.where(pair_lane < p_dim, outs[0], outs[1]))
        return [jnp.concatenate(pieces[d], axis=1) + y_off[d] for d in both]

    if has_h0:
        stf_ref[...] = h0_ref[0]
        stb_ref[...] = h0_ref[1]
    else:
        stf_ref[...] = jnp.zeros_like(stf_ref)
        stb_ref[...] = jnp.zeros_like(stb_ref)

    def skip_body(i, carry):
        rows = pl.ds(pl.multiple_of(i * q, q), q)
        ybuf_ref[rows, :] = xs_ref[rows, :].astype(F32) * dsk_ref[...]
        return carry

    lax.fori_loop(0, nc, skip_body, 0)

    def tables_body(i, carry):
        decay_tables([i * table_unroll + k for k in range(table_unroll)])
        return carry

    lax.fori_loop(0, nc // table_unroll, tables_body, 0)

    def scan_body(i, carry):
        yf, yb = chunk_pair((i, nc - 1 - i))
        ybuf_ref[chunk_rows(i), :] += yf
        ybuf_ref[chunk_rows(nc - 1 - i), :] += yb
        return carry

    lax.fori_loop(0, nc, scan_body, 0)

    def out_body(i, carry):
        rows = pl.ds(pl.multiple_of(i * q, q), q)
        y = ybuf_ref[rows, :] * _silu(z_ref[rows, :].astype(F32))
        y_ref[rows, :] = _rms(y, nw_ref[...]).astype(y_ref.dtype)
        return carry

    lax.fori_loop(0, nc, out_body, 0)

    if emit_state:
        st_out_ref[0] = stf_ref[...]
        st_out_ref[1] = stb_ref[...]


def _ssd_scan(z, xbc, dt, dt_bias, a_head, d_skip, norm_w, h0, emit_state, d_inner):
    b, l, _ = z.shape
    g = SSD_GROUPS
    gp = d_inner // g
    n = D_STATE
    hpg = gp // SSD_HEAD_DIM
    assert gp % _LANE == 0 and l % SSD_CHUNK == 0 and n == _LANE
    nc = l // SSD_CHUNK
    xoff = d_inner // n
    in_specs = [
        pl.BlockSpec((None, l, gp), lambda i, j: (i, 0, j)),
        pl.BlockSpec((None, l, n), lambda i, j: (i, 0, xoff + j)),
        pl.BlockSpec((None, l, n), lambda i, j: (i, 0, xoff + g + j)),
        pl.BlockSpec((None, l, gp), lambda i, j: (i, 0, j)),
        pl.BlockSpec((None, None, l, 2 * hpg), lambda i, j: (i, j, 0, 0)),
        pl.BlockSpec((None, 1, 2 * hpg), lambda i, j: (j, 0, 0)),
        pl.BlockSpec((None, 1, 2 * hpg), lambda i, j: (j, 0, 0)),
        pl.BlockSpec((1, gp), lambda i, j: (0, j)),
        pl.BlockSpec((1, gp), lambda i, j: (0, j)),
    ]
    args = [xbc, xbc, xbc, z, dt, dt_bias, a_head, d_skip, norm_w]
    state_spec = pl.BlockSpec((None, 2, None, n, gp), lambda i, j: (i, 0, j, 0, 0))
    if h0 is not None:
        in_specs.append(state_spec)
        args.append(h0)
    out_shape = [jax.ShapeDtypeStruct((b, l, d_inner), BF16)]
    out_specs = [pl.BlockSpec((None, l, gp), lambda i, j: (i, 0, j))]
    if emit_state:
        out_shape.append(jax.ShapeDtypeStruct((b, 2, g, n, gp), F32))
        out_specs.append(state_spec)
    outs = pl.pallas_call(
        functools.partial(_ssd_scan_kernel, has_h0=h0 is not None, emit_state=emit_state,
                          hpg=hpg, p_dim=SSD_HEAD_DIM),
        out_shape=out_shape,
        grid=(b, g),
        in_specs=in_specs,
        out_specs=out_specs,
        scratch_shapes=[pltpu.VMEM((l, gp), F32), pltpu.VMEM((n, gp), F32), pltpu.VMEM((n, gp), F32),
                        pltpu.VMEM((2, nc, SSD_CHUNK, hpg), F32), pltpu.VMEM((2, nc, hpg, SSD_CHUNK), F32),
                        pltpu.VMEM((2, nc, 3 * SSD_CHUNK + 16, hpg), BF16)],
        compiler_params=_params("parallel", "parallel"),
        name="ssd_scan",
    )(*args)
    return outs[0], (outs[1] if emit_state else None)


def _ssd_mixer(u, bsz, seq, h0, emit_state, w_in, w_out, li, conv_w, conv_b, dt_bias, a_log, d_skip, norm_w):
    d_inner = w_out.shape[1]
    heads = d_inner // SSD_HEAD_DIM
    g = SSD_GROUPS
    hpg = heads // g
    gp = d_inner // g
    n_main = w_in.shape[-1] - 2 * heads
    z = _matmul(u, w_in, w_index=(li,), n=d_inner, out_dtype=BF16, tn=512).reshape(bsz, seq, d_inner)
    xbc = _matmul_conv(u, seq, w_in, (li,), d_inner, conv_w, conv_b, _silu).reshape(bsz, seq, n_main - d_inner)
    dt = _matmul(u, w_in, w_index=(li,), col0=n_main, out_dtype=F32, tn=2 * heads)
    dt = dt.reshape(bsz, seq, 2, g, hpg).transpose(0, 3, 1, 2, 4).reshape(bsz, g, seq, 2 * hpg)
    per_group = lambda v: v.astype(F32).reshape(2, g, hpg).transpose(1, 0, 2).reshape(g, 1, 2 * hpg)
    if h0 is not None:
        h0 = h0.astype(F32).reshape(bsz, 2, g, hpg, SSD_HEAD_DIM, D_STATE)
        h0 = h0.transpose(0, 1, 2, 5, 3, 4).reshape(bsz, 2, g, D_STATE, gp)
    y, st = _ssd_scan(z, xbc, dt, per_group(dt_bias), per_group(-jnp.exp(a_log.astype(F32))),
                      jnp.repeat(d_skip.astype(F32), SSD_HEAD_DIM).reshape(1, d_inner),
                      norm_w.astype(F32).reshape(1, d_inner), h0, emit_state, d_inner)
    out = _matmul(y.reshape(bsz * seq, d_inner), w_out, w_index=(li,), out_dtype=BF16, tm=512, tn=512)
    if st is not None:
        st = st.reshape(bsz, 2, g, D_STATE, hpg, SSD_HEAD_DIM).transpose(0, 1, 2, 4, 5, 3)
        st = st.reshape(bsz, 2, heads, SSD_HEAD_DIM, D_STATE)
    return out, st


def _hy_mlp_kernel(bands_ref, w1t_ref, w1c_ref, w1s_ref, b1_ref, w2_ref, b2_ref, w3_ref, b3_ref, fq_ref,
                   o_ref, *, n):
    idx = lax.broadcasted_iota(jnp.int32, (n, 1), 0).astype(F32)
    hdot = functools.partial(jnp.dot, precision=HIGHEST, preferred_element_type=F32)
    for direction in range(2):
        pos = idx if direction == 0 else (n - 1.0) - idx
        t = pos / (n - 1.0)
        ang = (2.0 * math.pi * pos / n) * bands_ref[...]
        h = t * w1t_ref[...] + hdot(jnp.cos(ang), w1c_ref[...]) - hdot(jnp.sin(ang), w1s_ref[...])
        h = jnp.sin(fq_ref[0:1, :] * (h + b1_ref[...]))
        h = jnp.sin(fq_ref[1:2, :] * (hdot(h, w2_ref[...]) + b2_ref[...]))
        h = jnp.sin(fq_ref[2:3, :] * (hdot(h, w3_ref[...]) + b3_ref[...]))
        o_ref[direction] = h


def _hy_filter_kernel(h_ref, w_ref, delta_ref, o_ref, *, n):
    direction = pl.program_id(0) % 2
    idx = lax.broadcasted_iota(jnp.int32, (n, 1), 0).astype(F32)
    pos = jnp.where(direction == 0, idx, (n - 1.0) - idx)
    t = pos / (n - 1.0)
    k = jnp.dot(h_ref[...], w_ref[...], precision=HIGHEST, preferred_element_type=F32)
    k = k * jnp.exp(-t * delta_ref[...])
    o_ref[...] = k / (jnp.sum(jnp.abs(k), axis=0, keepdims=True) + RMS_EPS)


def _hy_spectrum_kernel(kf_ref, kb_ref, c_ref, s_ref, kp_ref, kq_ref, kn_ref, kf16_ref, kb16_ref, *, n, rchunk):
    dot = functools.partial(jnp.dot, preferred_element_type=F32)
    chunks = [slice(r0, r0 + rchunk) for r0 in range(0, n, rchunk)]
    alt = _alt_sign(0, rchunk)
    sign_n = 1.0 if n % 2 == 0 else -1.0
    nyq = jnp.zeros((1, kf_ref.shape[1]), F32)
    for rs in chunks:
        kf, kb = kf_ref[rs, :], kb_ref[rs, :]
        kf16_ref[rs, :] = kf.astype(BF16)
        kb16_ref[rs, :] = kb.astype(BF16)
        nyq = nyq + jnp.sum(alt * kf, axis=0, keepdims=True) + sign_n * jnp.sum(alt * kb, axis=0, keepdims=True)
    kn_ref[...] = nyq / (2.0 * n)
    for fs in chunks:
        freq = fs.start + lax.broadcasted_iota(jnp.int32, (rchunk, 1), 0)
        wgt = jnp.where(freq == 0, 1.0, 2.0) / (2.0 * n)
        kp_ref[fs, :] = (dot(c_ref[fs, :], kf16_ref[...]) + alt * dot(c_ref[fs, :], kb16_ref[...])) * wgt
        kq_ref[fs, :] = (dot(s_ref[fs, :], kf16_ref[...]) + alt * dot(s_ref[fs, :], kb16_ref[...])) * wgt


def _alt_sign(r0, rows):
    row = lax.broadcasted_iota(jnp.int32, (rows, 1), 0)
    return jnp.where(row % 2 == 0, 1.0, -1.0)


def _hy_conv_kernel(x1_ref, x2_ref, v_ref, kp_ref, kq_ref, kn_ref, bias_ref, c_ref, s_ref, o_ref,
                    z_ref, z16_ref, p2_ref, q2_ref, *, n, rchunk):
    dot = functools.partial(jnp.dot, preferred_element_type=F32)
    chunks = [slice(r0, r0 + rchunk) for r0 in range(0, n, rchunk)]
    alt = _alt_sign(0, rchunk)
    gate_refs = (x1_ref, x2_ref)
    for o in range(2):
        nyq = jnp.zeros((1, z_ref.shape[1]), F32)
        for rs in chunks:
            z = v_ref[rs, :].astype(F32) if o == 0 else z_ref[rs, :]
            if o == 0:
                z_ref[rs, :] = z
            z16_ref[rs, :] = z.astype(BF16)
            nyq = nyq + jnp.sum(alt * z, axis=0, keepdims=True)
        nyq = nyq * kn_ref[o]
        for fs in chunks:
            p = dot(c_ref[fs, :], z16_ref[...])
            q = dot(s_ref[fs, :], z16_ref[...])
            kp, kq = kp_ref[o, fs, :], kq_ref[o, fs, :]
            p2_ref[fs, :] = (p * kp - q * kq).astype(BF16)
            q2_ref[fs, :] = (p * kq + q * kp).astype(BF16)
        for rs in chunks:
            zc = dot(c_ref[rs, :], p2_ref[...]) + dot(s_ref[rs, :], q2_ref[...]) + alt * nyq
            z = gate_refs[o][rs, :].astype(F32) * (zc + z_ref[rs, :] * bias_ref[pl.ds(o, 1), :])
            if o == 0:
                z_ref[rs, :] = z
            else:
                o_ref[rs, :] = z.astype(o_ref.dtype)


def _dft_tables(n):
    f = lax.broadcasted_iota(jnp.int32, (n, n), 0)
    s = lax.broadcasted_iota(jnp.int32, (n, n), 1)
    ang = ((f * s) % (2 * n)).astype(F32) * (math.pi / n)
    return jnp.cos(ang).astype(BF16), jnp.sin(ang).astype(BF16)


def _resident(shape):
    return pl.BlockSpec(shape, lambda *_: (0,) * len(shape), pipeline_mode=pl.Buffered(1))


def _hyena_spectra(n, d, f_w1, f_b1, f_w2, f_b2, f_w3, f_b3, f_freq, f_w_out, tables, tc=512):
    fw = f_w1.shape[1]
    bands = jnp.linspace(1e-4, HY_BANDS - 1, HY_BANDS, dtype=F32).reshape(1, HY_BANDS)
    f32 = lambda a: a.astype(F32)
    row = lambda a: f32(a).reshape(1, -1)
    hdn = pl.pallas_call(
        functools.partial(_hy_mlp_kernel, n=n),
        out_shape=jax.ShapeDtypeStruct((2, n, fw), F32),
        name="hy_mlp",
    )(bands, f32(f_w1[0:1]), f32(f_w1[1:1 + HY_BANDS]), f32(f_w1[1 + HY_BANDS:]), row(f_b1),
      f32(f_w2), row(f_b2), f32(f_w3), row(f_b3), f32(f_freq))
    deltas = jnp.abs(jnp.linspace(math.log(HY_DECAY_TARGET) / HY_SLOW_DECAY,
                                  math.log(HY_DECAY_TARGET) / HY_FAST_DECAY, d, dtype=F32)).reshape(1, d)
    tc = min(tc, d)
    nj = d // tc
    k = pl.pallas_call(
        functools.partial(_hy_filter_kernel, n=n),
        out_shape=jax.ShapeDtypeStruct((4, n, d), F32),
        grid=(4, nj),
        in_specs=[pl.BlockSpec((None, n, fw), lambda i, j: (i % 2, 0, 0)),
                  pl.BlockSpec((fw, tc), lambda i, j: (0, i * nj + j)),
                  pl.BlockSpec((1, tc), lambda i, j: (0, j))],
        out_specs=pl.BlockSpec((None, n, tc), lambda i, j: (i, 0, j)),
        compiler_params=_params("parallel", "parallel"),
        name="hy_filter",
    )(hdn, f32(f_w_out), deltas)
    cos_t, sin_t = tables
    tcs = min(256, d)
    kp, kq, kn = pl.pallas_call(
        functools.partial(_hy_spectrum_kernel, n=n, rchunk=min(_HY_ROWS, n)),
        out_shape=[jax.ShapeDtypeStruct((2, n, d), F32), jax.ShapeDtypeStruct((2, n, d), F32),
                   jax.ShapeDtypeStruct((2, 1, d), F32)],
        grid=(2, d // tcs),
        in_specs=[pl.BlockSpec((None, n, tcs), lambda o, j: (2 * o, 0, j)),
                  pl.BlockSpec((None, n, tcs), lambda o, j: (2 * o + 1, 0, j)),
                  _resident((n, n)), _resident((n, n))],
        out_specs=[pl.BlockSpec((None, n, tcs), lambda o, j: (o, 0, j)),
                   pl.BlockSpec((None, n, tcs), lambda o, j: (o, 0, j)),
                   pl.BlockSpec((None, 1, tcs), lambda o, j: (o, 0, j))],
        scratch_shapes=[pltpu.VMEM((n, tcs), BF16), pltpu.VMEM((n, tcs), BF16)],
        compiler_params=_params("parallel", "parallel"),
        name="hy_spectrum",
    )(k, k, cos_t, sin_t)
    return kp, kq, kn


def _hyena_conv(proj, spectra, bias, tables, tc=256):
    b, n, d3 = proj.shape
    d = d3 // 3
    tc = min(tc, d)
    nj = d // tc
    kp, kq, kn = spectra
    cos_t, sin_t = tables
    part = lambda p: pl.BlockSpec((None, n, tc), lambda j, i: (i, 0, p * nj + j))
    spec = pl.BlockSpec((2, n, tc), lambda j, i: (0, 0, j), pipeline_mode=pl.Buffered(1))
    return pl.pallas_call(
        functools.partial(_hy_conv_kernel, n=n, rchunk=min(_HY_ROWS, n)),
        out_shape=jax.ShapeDtypeStruct((b, n, d), BF16),
        grid=(nj, b),
        in_specs=[part(0), part(1), part(2), spec, spec,
                  pl.BlockSpec((2, 1, tc), lambda j, i: (0, 0, j)),
                  pl.BlockSpec((2, tc), lambda j, i: (0, j)),
                  _resident((n, n)), _resident((n, n))],
        out_specs=pl.BlockSpec((None, n, tc), lambda j, i: (i, 0, j)),
        scratch_shapes=[pltpu.VMEM((n, tc), F32), pltpu.VMEM((n, tc), BF16),
                        pltpu.VMEM((n, tc), BF16), pltpu.VMEM((n, tc), BF16)],
        compiler_params=_params("parallel", "parallel"),
        name="hy_conv",
    )(proj, proj, proj, kp, kq, kn, bias.astype(F32), cos_t, sin_t)


def _hyena_mixer(u, bsz, seq, w_in, w_out, li, b_in, short_w, short_b, spectra, bias, b_out, tables):
    proj = _matmul_conv(u, seq, w_in, (li,), 0, short_w, short_b, lambda y: y, bias=b_in)
    z = _hyena_conv(proj.reshape(bsz, seq, -1), spectra, bias, tables)
    return _matmul(z.reshape(bsz * seq, -1), w_out, w_index=(li,), bias=b_out, out_dtype=BF16, tn=512)


def _latent_pos_embed(n_tok, d):
    rows = n_tok // GRID_W
    r = jnp.repeat(jnp.arange(rows, dtype=F32), GRID_W)
    col = jnp.tile(jnp.arange(GRID_W, dtype=F32), rows)
    quarter = d // 4
    omega = 1.0 / (10000.0 ** (jnp.arange(quarter, dtype=F32) / quarter))
    ar = r[:, None] * omega[None]
    ac = col[:, None] * omega[None]
    return jnp.concatenate([jnp.sin(ar), jnp.cos(ar), jnp.sin(ac), jnp.cos(ac)], axis=-1)


def kernel(x_prompt, x_sample, state_ssd, c, c_ctx, w_mod, b_mod, g_pre, g_post, ffn_w_gate, ffn_w_up, ffn_w_down, ssd_w_in, ssd_conv_w, ssd_conv_b, ssd_dt_bias, ssd_a_log, ssd_d, ssd_norm, ssd_w_out, hy_w_in, hy_b_in, hy_short_w, hy_short_b, hy_f_w1, hy_f_b1, hy_f_w2, hy_f_b2, hy_f_w3, hy_f_b3, hy_f_freq, hy_f_w_out, hy_bias, hy_w_out, hy_b_out):
    depth, d, _ = w_mod.shape
    ffn_w_down16, ssd_w_out16 = ffn_w_down.astype(BF16), ssd_w_out.astype(BF16)

    streams = []
    for x, rows in ((x_prompt, slice(0, 1)), (x_sample, slice(1, 1 + c.shape[0]))):
        bsz, seq, _ = x.shape
        streams.append(dict(bsz=bsz, seq=seq, x=x.reshape(bsz * seq, d), rows=rows))
    ctx, lat = streams

    cvec = jnp.concatenate([c_ctx[None], c], axis=0)
    n_cond = cvec.shape[0]
    cvec = jnp.pad(cvec, ((0, -n_cond % 8), (0, 0)))
    mod = _modulation(cvec, w_mod, b_mod).reshape(depth, -1, 3 * N_SUB, 1, d)

    def mod_vec(i, s, j, which):
        return mod[i, s["rows"], 3 * j + which]

    def pre_args(i, s, j):
        return (mod_vec(i, s, j, 0), mod_vec(i, s, j, 1), g_pre[i, j].reshape(1, d))

    def post_args(i, s, j, o, weight):
        return (o, mod_vec(i, s, j, 2), g_post[i, j].reshape(1, d), weight)

    tables = {s["seq"]: _dft_tables(s["seq"]) for s in streams} if depth > 1 else {}

    pos = _latent_pos_embed(lat["seq"], d)
    _, ctx["u"] = _post_pre(ctx["x"], ctx["seq"], pre=pre_args(0, ctx, 0))
    lat["x"], lat["u"] = _post_pre(lat["x"], lat["seq"], pos=pos, pre=pre_args(0, lat, 0))

    new_states = []
    for i in range(depth):
        kind, li = i % 2, i // 2
        for j in range(N_SUB):
            if j == 1 and kind == 0:
                outs = []
                for s, h0, emit in ((ctx, None, True), (lat, state_ssd[:, li], False)):
                    o, st = _ssd_mixer(s["u"], s["bsz"], s["seq"], h0, emit, ssd_w_in, ssd_w_out16, li,
                                       ssd_conv_w[li], ssd_conv_b[li], ssd_dt_bias[li], ssd_a_log[li],
                                       ssd_d[li], ssd_norm[li])
                    outs.append(o)
                    if emit:
                        new_states.append(st.astype(x_prompt.dtype))
                weight = 1.0
            elif j == 1:
                outs = []
                for s in streams:
                    spectra = _hyena_spectra(s["seq"], d, hy_f_w1[li], hy_f_b1[li], hy_f_w2[li], hy_f_b2[li],
                                             hy_f_w3[li], hy_f_b3[li], hy_f_freq[li], hy_f_w_out[li],
                                             tables[s["seq"]])
                    outs.append(_hyena_mixer(s["u"], s["bsz"], s["seq"], hy_w_in, hy_w_out, li, hy_b_in[li],
                                             hy_short_w[li], hy_short_b[li], spectra, hy_bias[li], hy_b_out[li],
                                             tables[s["seq"]]))
                weight = 1.0
            else:
                slot = (i, j // 2)
                outs = [_matmul(_gate_up(s["u"], ffn_w_gate, ffn_w_up, slot), ffn_w_down16, w_index=slot,
                                out_dtype=BF16, tm=512, tn=512) for s in streams]
                weight = 0.5
            nxt = (i, j + 1) if j + 1 < N_SUB else ((i + 1, 0) if i + 1 < depth else None)
            for s, o in zip(streams, outs):
                pre = pre_args(nxt[0], s, nxt[1]) if nxt is not None else None
                s["x"], s["u"] = _post_pre(s["x"], s["seq"], post=post_args(i, s, j, o, weight), pre=pre)

    y_prompt = ctx["x"].reshape(x_prompt.shape)
    y_sample = lat["x"].reshape(x_sample.shape)
    return (y_prompt, y_sample, jnp.stack(new_states, axis=1))
```

```python
import functools
import math

import jax
import jax.numpy as jnp
from jax import lax
from jax.experimental import pallas as pl
from jax.experimental.pallas import tpu as pltpu

F32 = jnp.float32
BF16 = jnp.bfloat16
HIGHEST = lax.Precision.HIGHEST

RMS_EPS = 1e-6
N_SUB = 3
GRID_W = 64

SSD_HEAD_DIM = 64
SSD_GROUPS = 8
D_STATE = 128
SSD_CHUNK = 128

HY_BANDS = 16
HY_DECAY_TARGET = 1e-2
HY_FAST_DECAY = 0.3
HY_SLOW_DECAY = 1.5

_V7X_VMEM_BYTES = 64 * 1024 * 1024
_VMEM_LIMIT = _V7X_VMEM_BYTES - 8 * 1024 * 1024
_LANE = 128


def _params(*semantics):
    return pltpu.CompilerParams(dimension_semantics=semantics, vmem_limit_bytes=_VMEM_LIMIT)


def _tile(dim, target):
    if dim <= target:
        return dim
    for t in range(target - target % _LANE, 0, -_LANE):
        if dim % t == 0:
            return t
    raise ValueError(f"no {_LANE}-aligned tile <= {target} divides {dim}")


def _nt_dot(a, b, **kw):
    return lax.dot_general(a, b, (((1,), (1,)), ((), ())), preferred_element_type=F32, **kw)


def _silu(x):
    return x * jax.nn.sigmoid(x)


def _softplus(x):
    return jnp.maximum(x, 0.0) + jnp.log1p(jnp.exp(-jnp.abs(x)))


def _mod_kernel(c_ref, w_ref, b_ref, o_ref):
    a = _silu(c_ref[...]).astype(BF16)
    o_ref[...] = jnp.dot(a, w_ref[...].astype(BF16), preferred_element_type=F32) + b_ref[...]


def _modulation(cvec, w_mod, b_mod, tn=512):
    depth, d, n = w_mod.shape
    r = cvec.shape[0]
    return pl.pallas_call(
        _mod_kernel,
        out_shape=jax.ShapeDtypeStruct((depth, r, n), F32),
        grid=(depth, n // tn),
        in_specs=[
            pl.BlockSpec((r, d), lambda l, j: (0, 0)),
            pl.BlockSpec((None, d, tn), lambda l, j: (l, 0, j)),
            pl.BlockSpec((None, 1, tn), lambda l, j: (l, 0, j)),
        ],
        out_specs=pl.BlockSpec((None, r, tn), lambda l, j: (l, 0, j)),
        compiler_params=_params("parallel", "parallel"),
        name="modulation",
    )(cvec, w_mod, b_mod.reshape(depth, 1, n))


def _rms(v, g):
    return v * lax.rsqrt(jnp.mean(v * v, axis=-1, keepdims=True) + RMS_EPS) * g


def _post_pre_kernel(*refs, has_pos, has_post, has_pre, weight):
    refs = list(refs)
    x = refs.pop(0)[...]
    if has_pos:
        x = x + refs.pop(0)[...]
    if has_post:
        o_ref, gate_ref, gpost_ref = refs.pop(0), refs.pop(0), refs.pop(0)
        x = x + weight * gate_ref[...] * _rms(o_ref[...].astype(F32), gpost_ref[...])
    if has_pre:
        shift_ref, scale_ref, gpre_ref = refs.pop(0), refs.pop(0), refs.pop(0)
        u = _rms(x, gpre_ref[...]) * (1.0 + scale_ref[...]) + shift_ref[...]
    if has_pos or has_post:
        refs.pop(0)[...] = x
    if has_pre:
        refs.pop(0)[...] = u.astype(BF16)


def _post_pre(x, seq_len, *, pos=None, post=None, pre=None, tm=256):
    t, d = x.shape
    tm = min(tm, seq_len)
    assert seq_len % tm == 0 and t % tm == 0

    def mod_spec(m):
        if m.shape[0] == 1:
            return pl.BlockSpec((None, 1, d), lambda i: (0, 0, 0))
        return pl.BlockSpec((None, 1, d), lambda i: ((i * tm) // seq_len, 0, 0))

    row_spec = pl.BlockSpec((tm, d), lambda i: (i, 0))
    vec_spec = pl.BlockSpec((1, d), lambda i: (0, 0))
    args, in_specs = [x], [row_spec]
    if pos is not None:
        per_seq = seq_len // tm
        args.append(pos)
        in_specs.append(pl.BlockSpec((tm, d), lambda i: (i % per_seq, 0)))
    weight = 0.0
    if post is not None:
        o, gate, g_post, weight = post
        args += [o, gate, g_post]
        in_specs += [row_spec, mod_spec(gate), vec_spec]
    if pre is not None:
        shift, scale, g_pre = pre
        args += [shift, scale, g_pre]
        in_specs += [mod_spec(shift), mod_spec(scale), vec_spec]
    out_shape, out_specs = [], []
    new_x = pos is not None or post is not None
    if new_x:
        out_shape.append(jax.ShapeDtypeStruct((t, d), F32))
        out_specs.append(row_spec)
    if pre is not None:
        out_shape.append(jax.ShapeDtypeStruct((t, d), BF16))
        out_specs.append(row_spec)
    outs = pl.pallas_call(
        functools.partial(_post_pre_kernel, has_pos=pos is not None, has_post=post is not None,
                          has_pre=pre is not None, weight=weight),
        out_shape=out_shape,
        grid=(t // tm,),
        in_specs=in_specs,
        out_specs=out_specs,
        compiler_params=_params("parallel"),
        name="post_pre",
    )(*args)
    outs = list(outs)
    x_new = outs.pop(0) if new_x else None
    u = outs.pop(0) if pre is not None else None
    return x_new, u


def _mm_kernel(*refs, has_bias):
    refs = list(refs)
    x_ref, w_ref = refs.pop(0), refs.pop(0)
    b_ref = refs.pop(0) if has_bias else None
    o_ref = refs.pop(0)
    acc = jnp.dot(x_ref[...], w_ref[...].astype(BF16), preferred_element_type=F32)
    if has_bias:
        acc = acc + b_ref[...]
    o_ref[...] = acc.astype(o_ref.dtype)


def _row_tile_spec(tm, kdim):
    return pl.BlockSpec((tm, kdim), lambda i, j: (i, 0), pipeline_mode=pl.Buffered(1))


def _w_spec(w, w_index, tn, col_block0):
    lead = tuple(w_index)
    assert len(lead) == w.ndim - 2
    return pl.BlockSpec((None,) * len(lead) + (w.shape[-2], tn), lambda i, j: lead + (0, col_block0 + j))


def _matmul(x, w, *, w_index=(), col0=0, n=None, bias=None, out_dtype=F32, tm=1024, tn=1024):
    t, kdim = x.shape
    n = w.shape[-1] - col0 if n is None else n
    assert w.shape[-2] == kdim
    tm, tn = _tile(t, tm), _tile(n, tn)
    if tn % _LANE or col0 % tn:
        w, w_index, col0 = w[tuple(w_index)][:, col0:col0 + n], (), 0
    args = [x, w]
    in_specs = [pl.BlockSpec((tm, kdim), lambda i, j: (i, 0)), _w_spec(w, w_index, tn, col0 // tn)]
    if bias is not None:
        args.append(bias.reshape(1, n).astype(F32))
        in_specs.append(pl.BlockSpec((1, tn), lambda i, j: (0, j)))
    return pl.pallas_call(
        functools.partial(_mm_kernel, has_bias=bias is not None),
        out_shape=jax.ShapeDtypeStruct((t, n), out_dtype),
        grid=(t // tm, n // tn),
        in_specs=in_specs,
        out_specs=pl.BlockSpec((tm, tn), lambda i, j: (i, j)),
        compiler_params=_params("parallel", "parallel"),
        name="matmul",
    )(*args)


def _mm_conv_kernel(*refs, seq, act, has_bias):
    refs = list(refs)
    x_ref, w_ref = refs.pop(0), refs.pop(0)
    b_ref = refs.pop(0) if has_bias else None
    cw_ref, cb_ref, o_ref = refs
    acc = jnp.dot(x_ref[...], w_ref[...].astype(BF16), preferred_element_type=F32)
    if has_bias:
        acc = acc + b_ref[...]
    tm = acc.shape[0]
    k_w = cw_ref.shape[0]
    pos = lax.rem(lax.broadcasted_iota(jnp.int32, (tm, 1), 0), seq)
    y = cb_ref[...] + acc * cw_ref[pl.ds(k_w // 2, 1), :]
    for k in range(k_w):
        d = k - k_w // 2
        if d != 0:
            shifted = pltpu.roll(acc, (-d) % tm, 0)
            inside = (pos + d >= 0) & (pos + d < seq)
            y = y + jnp.where(inside, shifted, 0.0) * cw_ref[pl.ds(k, 1), :]
    o_ref[...] = act(y).astype(o_ref.dtype)


def _matmul_conv(x, seq, w, w_index, col0, conv_w, conv_b, act, bias=None, tn=256, rows=1024):
    t, kdim = x.shape
    k_w, c = conv_w.shape
    tm = seq * max(1, min(rows, t) // seq)
    tn = _tile(c, tn)
    assert t % tm == 0 and col0 % tn == 0 and w.shape[-2] == kdim
    args = [x, w]
    in_specs = [_row_tile_spec(tm, kdim), _w_spec(w, w_index, tn, col0 // tn)]
    col_spec = lambda r: pl.BlockSpec((r, tn), lambda i, j: (0, j))
    if bias is not None:
        args.append(bias.reshape(1, c).astype(F32))
        in_specs.append(col_spec(1))
    args += [conv_w.astype(F32), conv_b.reshape(1, c).astype(F32)]
    in_specs += [col_spec(k_w), col_spec(1)]
    return pl.pallas_call(
        functools.partial(_mm_conv_kernel, seq=seq, act=act, has_bias=bias is not None),
        out_shape=jax.ShapeDtypeStruct((t, c), BF16),
        grid=(t // tm, c // tn),
        in_specs=in_specs,
        out_specs=pl.BlockSpec((tm, tn), lambda i, j: (i, j)),
        compiler_params=_params("parallel", "parallel"),
        name="matmul_conv",
    )(*args)


def _gate_up_kernel(u_ref, wg_ref, wu_ref, o_ref):
    u = u_ref[...]
    g = jnp.dot(u, wg_ref[...].astype(BF16), preferred_element_type=F32)
    v = jnp.dot(u, wu_ref[...].astype(BF16), preferred_element_type=F32)
    o_ref[...] = (_silu(g) * v).astype(o_ref.dtype)


def _gate_up(u, w_gate, w_up, w_index, tm=2048, tn=256):
    t, d = u.shape
    n = w_gate.shape[-1]
    tm, tn = _tile(t, tm), _tile(n, tn)
    return pl.pallas_call(
        _gate_up_kernel,
        out_shape=jax.ShapeDtypeStruct((t, n), BF16),
        grid=(t // tm, n // tn),
        in_specs=[_row_tile_spec(tm, d), _w_spec(w_gate, w_index, tn, 0), _w_spec(w_up, w_index, tn, 0)],
        out_specs=pl.BlockSpec((tm, tn), lambda i, j: (i, j)),
        compiler_params=_params("parallel", "parallel"),
        name="gate_up",
    )(u, w_gate, w_up)


_HY_ROWS = 512


def _ssd_scan_kernel(*refs, has_h0, emit_state, hpg, p_dim):
    refs = list(refs)
    xs_ref, b_ref, c_ref, z_ref, dt_ref, dtb_ref, a_ref, dsk_ref, nw_ref = refs[:9]
    refs = refs[9:]
    h0_ref = refs.pop(0) if has_h0 else None
    y_ref = refs.pop(0)
    st_out_ref = refs.pop(0) if emit_state else None
    ybuf_ref, stf_ref, stb_ref, cs_ref, cst_ref, fac_ref = refs

    q = SSD_CHUNK
    seq, gp = xs_ref.shape
    nc = seq // q
    table_unroll = math.gcd(nc, 4)
    t_i = lax.broadcasted_iota(jnp.int32, (q, q), 0)
    s_i = lax.broadcasted_iota(jnp.int32, (q, q), 1)
    masks = (s_i <= t_i, s_i >= t_i)
    tris = (masks[0].astype(F32), masks[1].astype(F32))
    eye = (lax.broadcasted_iota(jnp.int32, (hpg, hpg), 0)
           == lax.broadcasted_iota(jnp.int32, (hpg, hpg), 1)).astype(F32)
    expand_bf = (lax.broadcasted_iota(jnp.int32, (hpg, gp), 1) // p_dim
                 == lax.broadcasted_iota(jnp.int32, (hpg, gp), 0)).astype(F32).astype(BF16)
    pair_lane = lax.broadcasted_iota(jnp.int32, (q, 2 * p_dim), 1)

    st_refs = (stf_ref, stb_ref)
    both = (0, 1)

    def chunk_rows(c):
        return pl.ds(pl.multiple_of(c * q, q), q)

    def decay_tables(chunks):
        jobs = [(c, d) for c in chunks for d in both]
        hs = [slice(d * hpg, (d + 1) * hpg) for d in both]
        dt = [_softplus(dt_ref[chunk_rows(c), :][:, hs[d]] + dtb_ref[...][:, hs[d]]) for c, d in jobs]
        a = [dt[k] * a_ref[...][:, hs[d]] for k, (c, d) in enumerate(jobs)]
        cs = [jnp.dot(tris[d], a[k], precision=HIGHEST, preferred_element_type=F32) for k, (c, d) in enumerate(jobs)]
        cs_t = [_nt_dot(eye, cs[k], precision=HIGHEST) for k in range(len(jobs))]
        end = [cs[k][q - 1:q, :] if d == 0 else cs[k][0:1, :] for k, (c, d) in enumerate(jobs)]
        for k, (c, d) in enumerate(jobs):
            cs_ref[d, c] = cs[k]
            cst_ref[d, c] = cs_t[k]
            dec = jnp.broadcast_to(jnp.exp(end[k]), (8, hpg))
            dec_hi = dec.astype(BF16)
            dec_lo = (dec - dec_hi.astype(F32)).astype(BF16)
            rows3 = jnp.concatenate([dt[k], dt[k] * jnp.exp(end[k] - cs[k]), jnp.exp(cs[k])], axis=0).astype(BF16)
            fac_ref[d, c] = jnp.concatenate([rows3, dec_hi, dec_lo], axis=0)

    def chunk_pair(cidx):
        rows = [chunk_rows(c) for c in cidx]
        cs = [cs_ref[d, cidx[d]] for d in both]
        cs_t = [cst_ref[d, cidx[d]] for d in both]
        fx = [jnp.dot(fac_ref[d, cidx[d]], expand_bf, preferred_element_type=F32) for d in both]
        chunk_decay = [fx[d][3 * q:3 * q + 1] + fx[d][3 * q + 8:3 * q + 9] for d in both]
        bm = [b_ref[rows[d], :] for d in both]
        cm = [c_ref[rows[d], :] for d in both]
        cb = [_nt_dot(cm[d], bm[d]) for d in both]
        xs = [xs_ref[rows[d], :].astype(F32) for d in both]
        xg = [(xs[d] * fx[d][:q]).astype(BF16) for d in both]
        xd = [(xs[d] * fx[d][q:2 * q]).astype(BF16) for d in both]
        st = [st_refs[d][...] for d in both]
        y_off = [jnp.dot(cm[d], st[d].astype(BF16), preferred_element_type=F32) * fx[d][2 * q:3 * q] for d in both]
        b_t = [bm[d].astype(F32).T.astype(BF16) for d in both]
        for d in both:
            st_refs[d][...] = st[d] * chunk_decay[d] + jnp.dot(b_t[d], xd[d], preferred_element_type=F32)
        pieces = ([], [])
        for p in range(hpg // 2):
            for d in both:
                xg_pair = xg[d][:, 2 * p * p_dim:(2 * p + 2) * p_dim]
                outs = []
                for r in (2 * p, 2 * p + 1):
                    seg = cs[d][:, r:r + 1] - cs_t[d][r:r + 1, :]
                    m = (cb[d] * jnp.exp(jnp.where(masks[d], seg, -jnp.inf))).astype(BF16)
                    outs.append(jnp.dot(m, xg_pair, preferred_element_type=F32))
                pieces[d].append(jnp.where(pair_lane < p_dim, outs[0], outs[1]))
        return [jnp.concatenate(pieces[d], axis=1) + y_off[d] for d in both]

    if has_h0:
        stf_ref[...] = h0_ref[0]
        stb_ref[...] = h0_ref[1]
    else:
        stf_ref[...] = jnp.zeros_like(stf_ref)
        stb_ref[...] = jnp.zeros_like(stb_ref)

    def skip_body(i, carry):
        rows = pl.ds(pl.multiple_of(i * q, q), q)
        ybuf_ref[rows, :] = xs_ref[rows, :].astype(F32) * dsk_ref[...]
        return carry

    lax.fori_loop(0, nc, skip_body, 0)

    def tables_body(i, carry):
        decay_tables([i * table_unroll + k for k in range(table_unroll)])
        return carry

    lax.fori_loop(0, nc // table_unroll, tables_body, 0)

    def gate_norm(rows, y):
        y = y * _silu(z_ref[rows, :].astype(F32))
        y_ref[rows, :] = _rms(y, nw_ref[...]).astype(y_ref.dtype)

    def first_half(i, carry):
        yf, yb = chunk_pair((i, nc - 1 - i))
        ybuf_ref[chunk_rows(i), :] += yf
        ybuf_ref[chunk_rows(nc - 1 - i), :] += yb
        return carry

    def second_half(i, carry):
        yf, yb = chunk_pair((i, nc - 1 - i))
        gate_norm(chunk_rows(i), ybuf_ref[chunk_rows(i), :] + yf)
        gate_norm(chunk_rows(nc - 1 - i), ybuf_ref[chunk_rows(nc - 1 - i), :] + yb)
        return carry

    lax.fori_loop(0, nc // 2, first_half, 0)
    lax.fori_loop(nc // 2, nc, second_half, 0)

    if emit_state:
        st_out_ref[0] = stf_ref[...]
        st_out_ref[1] = stb_ref[...]


def _ssd_scan(z, xbc, dt, dt_bias, a_head, d_skip, norm_w, h0, emit_state, d_inner):
    b, l, _ = z.shape
    g = SSD_GROUPS
    gp = d_inner // g
    n = D_STATE
    hpg = gp // SSD_HEAD_DIM
    assert gp % _LANE == 0 and l % SSD_CHUNK == 0 and n == _LANE
    nc = l // SSD_CHUNK
    assert nc % 2 == 0
    xoff = d_inner // n
    in_specs = [
        pl.BlockSpec((None, l, gp), lambda i, j: (i, 0, j)),
        pl.BlockSpec((None, l, n), lambda i, j: (i, 0, xoff + j)),
        pl.BlockSpec((None, l, n), lambda i, j: (i, 0, xoff + g + j)),
        pl.BlockSpec((None, l, gp), lambda i, j: (i, 0, j)),
        pl.BlockSpec((None, None, l, 2 * hpg), lambda i, j: (i, j, 0, 0)),
        pl.BlockSpec((None, 1, 2 * hpg), lambda i, j: (j, 0, 0)),
        pl.BlockSpec((None, 1, 2 * hpg), lambda i, j: (j, 0, 0)),
        pl.BlockSpec((1, gp), lambda i, j: (0, j)),
        pl.BlockSpec((1, gp), lambda i, j: (0, j)),
    ]
    args = [xbc, xbc, xbc, z, dt, dt_bias, a_head, d_skip, norm_w]
    state_spec = pl.BlockSpec((None, 2, None, n, gp), lambda i, j: (i, 0, j, 0, 0))
    if h0 is not None:
        in_specs.append(state_spec)
        args.append(h0)
    out_shape = [jax.ShapeDtypeStruct((b, l, d_inner), BF16)]
    out_specs = [pl.BlockSpec((None, l, gp), lambda i, j: (i, 0, j))]
    if emit_state:
        out_shape.append(jax.ShapeDtypeStruct((b, 2, g, n, gp), F32))
        out_specs.append(state_spec)
    outs = pl.pallas_call(
        functools.partial(_ssd_scan_kernel, has_h0=h0 is not None, emit_state=emit_state,
                          hpg=hpg, p_dim=SSD_HEAD_DIM),
        out_shape=out_shape,
        grid=(b, g),
        in_specs=in_specs,
        out_specs=out_specs,
        scratch_shapes=[pltpu.VMEM((l, gp), F32), pltpu.VMEM((n, gp), F32), pltpu.VMEM((n, gp), F32),
                        pltpu.VMEM((2, nc, SSD_CHUNK, hpg), F32), pltpu.VMEM((2, nc, hpg, SSD_CHUNK), F32),
                        pltpu.VMEM((2, nc, 3 * SSD_CHUNK + 16, hpg), BF16)],
        compiler_params=_params("parallel", "parallel"),
        name="ssd_scan",
    )(*args)
    return outs[0], (outs[1] if emit_state else None)


def _ssd_mixer(u, bsz, seq, h0, emit_state, w_in, w_out, li, conv_w, conv_b, dt_bias, a_log, d_skip, norm_w):
    d_inner = w_out.shape[1]
    heads = d_inner // SSD_HEAD_DIM
    g = SSD_GROUPS
    hpg = heads // g
    gp = d_inner // g
    n_main = w_in.shape[-1] - 2 * heads
    z = _matmul(u, w_in, w_index=(li,), n=d_inner, out_dtype=BF16, tn=512).reshape(bsz, seq, d_inner)
    xbc = _matmul_conv(u, seq, w_in, (li,), d_inner, conv_w, conv_b, _silu).reshape(bsz, seq, n_main - d_inner)
    dt = _matmul(u, w_in, w_index=(li,), col0=n_main, out_dtype=F32, tn=2 * heads)
    dt = dt.reshape(bsz, seq, 2, g, hpg).transpose(0, 3, 1, 2, 4).reshape(bsz, g, seq, 2 * hpg)
    per_group = lambda v: v.astype(F32).reshape(2, g, hpg).transpose(1, 0, 2).reshape(g, 1, 2 * hpg)
    if h0 is not None:
        h0 = h0.astype(F32).reshape(bsz, 2, g, hpg, SSD_HEAD_DIM, D_STATE)
        h0 = h0.transpose(0, 1, 2, 5, 3, 4).reshape(bsz, 2, g, D_STATE, gp)
    y, st = _ssd_scan(z, xbc, dt, per_group(dt_bias), per_group(-jnp.exp(a_log.astype(F32))),
                      jnp.repeat(d_skip.astype(F32), SSD_HEAD_DIM).reshape(1, d_inner),
                      norm_w.astype(F32).reshape(1, d_inner), h0, emit_state, d_inner)
    out = _matmul(y.reshape(bsz * seq, d_inner), w_out, w_index=(li,), out_dtype=BF16, tm=512, tn=512)
    if st is not None:
        st = st.reshape(bsz, 2, g, D_STATE, hpg, SSD_HEAD_DIM).transpose(0, 1, 2, 4, 5, 3)
        st = st.reshape(bsz, 2, heads, SSD_HEAD_DIM, D_STATE)
    return out, st


def _hy_mlp_kernel(bands_ref, w1t_ref, w1c_ref, w1s_ref, b1_ref, w2_ref, b2_ref, w3_ref, b3_ref, fq_ref,
                   o_ref, *, n):
    idx = lax.broadcasted_iota(jnp.int32, (n, 1), 0).astype(F32)
    hdot = functools.partial(jnp.dot, precision=HIGHEST, preferred_element_type=F32)
    for direction in range(2):
        pos = idx if direction == 0 else (n - 1.0) - idx
        t = pos / (n - 1.0)
        ang = (2.0 * math.pi * pos / n) * bands_ref[...]
        h = t * w1t_ref[...] + hdot(jnp.cos(ang), w1c_ref[...]) - hdot(jnp.sin(ang), w1s_ref[...])
        h = jnp.sin(fq_ref[0:1, :] * (h + b1_ref[...]))
        h = jnp.sin(fq_ref[1:2, :] * (hdot(h, w2_ref[...]) + b2_ref[...]))
        h = jnp.sin(fq_ref[2:3, :] * (hdot(h, w3_ref[...]) + b3_ref[...]))
        o_ref[direction] = h


def _hy_filter_kernel(h_ref, w_ref, delta_ref, o_ref, *, n):
    direction = pl.program_id(0) % 2
    idx = lax.broadcasted_iota(jnp.int32, (n, 1), 0).astype(F32)
    pos = jnp.where(direction == 0, idx, (n - 1.0) - idx)
    t = pos / (n - 1.0)
    k = jnp.dot(h_ref[...], w_ref[...], precision=HIGHEST, preferred_element_type=F32)
    k = k * jnp.exp(-t * delta_ref[...])
    o_ref[...] = k / (jnp.sum(jnp.abs(k), axis=0, keepdims=True) + RMS_EPS)


def _hy_spectrum_kernel(kf_ref, kb_ref, c_ref, s_ref, kp_ref, kq_ref, kn_ref, kf16_ref, kb16_ref, *, n, rchunk):
    dot = functools.partial(jnp.dot, preferred_element_type=F32)
    chunks = [slice(r0, r0 + rchunk) for r0 in range(0, n, rchunk)]
    alt = _alt_sign(0, rchunk)
    sign_n = 1.0 if n % 2 == 0 else -1.0
    nyq = jnp.zeros((1, kf_ref.shape[1]), F32)
    for rs in chunks:
        kf, kb = kf_ref[rs, :], kb_ref[rs, :]
        kf16_ref[rs, :] = kf.astype(BF16)
        kb16_ref[rs, :] = kb.astype(BF16)
        nyq = nyq + jnp.sum(alt * kf, axis=0, keepdims=True) + sign_n * jnp.sum(alt * kb, axis=0, keepdims=True)
    kn_ref[...] = nyq / (2.0 * n)
    for fs in chunks:
        freq = fs.start + lax.broadcasted_iota(jnp.int32, (rchunk, 1), 0)
        wgt = jnp.where(freq == 0, 1.0, 2.0) / (2.0 * n)
        kp_ref[fs, :] = (dot(c_ref[fs, :], kf16_ref[...]) + alt * dot(c_ref[fs, :], kb16_ref[...])) * wgt
        kq_ref[fs, :] = (dot(s_ref[fs, :], kf16_ref[...]) + alt * dot(s_ref[fs, :], kb16_ref[...])) * wgt


def _alt_sign(r0, rows):
    row = lax.broadcasted_iota(jnp.int32, (rows, 1), 0)
    return jnp.where(row % 2 == 0, 1.0, -1.0)


def _hy_conv_kernel(x1_ref, x2_ref, v_ref, kp_ref, kq_ref, kn_ref, bias_ref, c_ref, s_ref, o_ref,
                    z_ref, z16_ref, p2_ref, q2_ref, *, n, rchunk):
    dot = functools.partial(jnp.dot, preferred_element_type=F32)
    chunks = [slice(r0, r0 + rchunk) for r0 in range(0, n, rchunk)]
    alt = _alt_sign(0, rchunk)
    gate_refs = (x1_ref, x2_ref)
    for o in range(2):
        nyq = jnp.zeros((1, z_ref.shape[1]), F32)
        for rs in chunks:
            z = v_ref[rs, :].astype(F32) if o == 0 else z_ref[rs, :]
            if o == 0:
                z_ref[rs, :] = z
            z16_ref[rs, :] = z.astype(BF16)
            nyq = nyq + jnp.sum(alt * z, axis=0, keepdims=True)
        nyq = nyq * kn_ref[o]
        for fs in chunks:
            p = dot(c_ref[fs, :], z16_ref[...])
            q = dot(s_ref[fs, :], z16_ref[...])
            kp, kq = kp_ref[o, fs, :], kq_ref[o, fs, :]
            p2_ref[fs, :] = (p * kp - q * kq).astype(BF16)
            q2_ref[fs, :] = (p * kq + q * kp).astype(BF16)
        for rs in chunks:
            zc = dot(c_ref[rs, :], p2_ref[...]) + dot(s_ref[rs, :], q2_ref[...]) + alt * nyq
            z = gate_refs[o][rs, :].astype(F32) * (zc + z_ref[rs, :] * bias_ref[pl.ds(o, 1), :])
            if o == 0:
                z_ref[rs, :] = z
            else:
                o_ref[rs, :] = z.astype(o_ref.dtype)


def _dft_tables(n):
    f = lax.broadcasted_iota(jnp.int32, (n, n), 0)
    s = lax.broadcasted_iota(jnp.int32, (n, n), 1)
    ang = ((f * s) % (2 * n)).astype(F32) * (math.pi / n)
    return jnp.cos(ang).astype(BF16), jnp.sin(ang).astype(BF16)


def _resident(shape):
    return pl.BlockSpec(shape, lambda *_: (0,) * len(shape), pipeline_mode=pl.Buffered(1))


def _hyena_spectra(n, d, f_w1, f_b1, f_w2, f_b2, f_w3, f_b3, f_freq, f_w_out, tables, tc=512):
    fw = f_w1.shape[1]
    bands = jnp.linspace(1e-4, HY_BANDS - 1, HY_BANDS, dtype=F32).reshape(1, HY_BANDS)
    f32 = lambda a: a.astype(F32)
    row = lambda a: f32(a).reshape(1, -1)
    hdn = pl.pallas_call(
        functools.partial(_hy_mlp_kernel, n=n),
        out_shape=jax.ShapeDtypeStruct((2, n, fw), F32),
        name="hy_mlp",
    )(bands, f32(f_w1[0:1]), f32(f_w1[1:1 + HY_BANDS]), f32(f_w1[1 + HY_BANDS:]), row(f_b1),
      f32(f_w2), row(f_b2), f32(f_w3), row(f_b3), f32(f_freq))
    deltas = jnp.abs(jnp.linspace(math.log(HY_DECAY_TARGET) / HY_SLOW_DECAY,
                                  math.log(HY_DECAY_TARGET) / HY_FAST_DECAY, d, dtype=F32)).reshape(1, d)
    tc = min(tc, d)
    nj = d // tc
    k = pl.pallas_call(
        functools.partial(_hy_filter_kernel, n=n),
        out_shape=jax.ShapeDtypeStruct((4, n, d), F32),
        grid=(4, nj),
        in_specs=[pl.BlockSpec((None, n, fw), lambda i, j: (i % 2, 0, 0)),
                  pl.BlockSpec((fw, tc), lambda i, j: (0, i * nj + j)),
                  pl.BlockSpec((1, tc), lambda i, j: (0, j))],
        out_specs=pl.BlockSpec((None, n, tc), lambda i, j: (i, 0, j)),
        compiler_params=_params("parallel", "parallel"),
        name="hy_filter",
    )(hdn, f32(f_w_out), deltas)
    cos_t, sin_t = tables
    tcs = min(256, d)
    kp, kq, kn = pl.pallas_call(
        functools.partial(_hy_spectrum_kernel, n=n, rchunk=min(_HY_ROWS, n)),
        out_shape=[jax.ShapeDtypeStruct((2, n, d), F32), jax.ShapeDtypeStruct((2, n, d), F32),
                   jax.ShapeDtypeStruct((2, 1, d), F32)],
        grid=(2, d // tcs),
        in_specs=[pl.BlockSpec((None, n, tcs), lambda o, j: (2 * o, 0, j)),
                  pl.BlockSpec((None, n, tcs), lambda o, j: (2 * o + 1, 0, j)),
                  _resident((n, n)), _resident((n, n))],
        out_specs=[pl.BlockSpec((None, n, tcs), lambda o, j: (o, 0, j)),
                   pl.BlockSpec((None, n, tcs), lambda o, j: (o, 0, j)),
                   pl.BlockSpec((None, 1, tcs), lambda o, j: (o, 0, j))],
        scratch_shapes=[pltpu.VMEM((n, tcs), BF16), pltpu.VMEM((n, tcs), BF16)],
        compiler_params=_params("parallel", "parallel"),
        name="hy_spectrum",
    )(k, k, cos_t, sin_t)
    return kp, kq, kn


def _hyena_conv(proj, spectra, bias, tables, tc=256):
    b, n, d3 = proj.shape
    d = d3 // 3
    tc = min(tc, d)
    nj = d // tc
    kp, kq, kn = spectra
    cos_t, sin_t = tables
    part = lambda p: pl.BlockSpec((None, n, tc), lambda j, i: (i, 0, p * nj + j))
    spec = pl.BlockSpec((2, n, tc), lambda j, i: (0, 0, j), pipeline_mode=pl.Buffered(1))
    return pl.pallas_call(
        functools.partial(_hy_conv_kernel, n=n, rchunk=min(_HY_ROWS, n)),
        out_shape=jax.ShapeDtypeStruct((b, n, d), BF16),
        grid=(nj, b),
        in_specs=[part(0), part(1), part(2), spec, spec,
                  pl.BlockSpec((2, 1, tc), lambda j, i: (0, 0, j)),
                  pl.BlockSpec((2, tc), lambda j, i: (0, j)),
                  _resident((n, n)), _resident((n, n))],
        out_specs=pl.BlockSpec((None, n, tc), lambda j, i: (i, 0, j)),
        scratch_shapes=[pltpu.VMEM((n, tc), F32), pltpu.VMEM((n, tc), BF16),
                        pltpu.VMEM((n, tc), BF16), pltpu.VMEM((n, tc), BF16)],
        compiler_params=_params("parallel", "parallel"),
        name="hy_conv",
    )(proj, proj, proj, kp, kq, kn, bias.astype(F32), cos_t, sin_t)


def _hyena_mixer(u, bsz, seq, w_in, w_out, li, b_in, short_w, short_b, spectra, bias, b_out, tables):
    proj = _matmul_conv(u, seq, w_in, (li,), 0, short_w, short_b, lambda y: y, bias=b_in)
    z = _hyena_conv(proj.reshape(bsz, seq, -1), spectra, bias, tables)
    return _matmul(z.reshape(bsz * seq, -1), w_out, w_index=(li,), bias=b_out, out_dtype=BF16, tn=512)


def _latent_pos_embed(n_tok, d):
    rows = n_tok // GRID_W
    r = jnp.repeat(jnp.arange(rows, dtype=F32), GRID_W)
    col = jnp.tile(jnp.arange(GRID_W, dtype=F32), rows)
    quarter = d // 4
    omega = 1.0 / (10000.0 ** (jnp.arange(quarter, dtype=F32) / quarter))
    ar = r[:, None] * omega[None]
    ac = col[:, None] * omega[None]
    return jnp.concatenate([jnp.sin(ar), jnp.cos(ar), jnp.sin(ac), jnp.cos(ac)], axis=-1)


def kernel(x_prompt, x_sample, state_ssd, c, c_ctx, w_mod, b_mod, g_pre, g_post, ffn_w_gate, ffn_w_up, ffn_w_down, ssd_w_in, ssd_conv_w, ssd_conv_b, ssd_dt_bias, ssd_a_log, ssd_d, ssd_norm, ssd_w_out, hy_w_in, hy_b_in, hy_short_w, hy_short_b, hy_f_w1, hy_f_b1, hy_f_w2, hy_f_b2, hy_f_w3, hy_f_b3, hy_f_freq, hy_f_w_out, hy_bias, hy_w_out, hy_b_out):
    depth, d, _ = w_mod.shape
    ffn_w_down16, ssd_w_out16 = ffn_w_down.astype(BF16), ssd_w_out.astype(BF16)

    streams = []
    for x, rows in ((x_prompt, slice(0, 1)), (x_sample, slice(1, 1 + c.shape[0]))):
        bsz, seq, _ = x.shape
        streams.append(dict(bsz=bsz, seq=seq, x=x.reshape(bsz * seq, d), rows=rows))
    ctx, lat = streams

    cvec = jnp.concatenate([c_ctx[None], c], axis=0)
    n_cond = cvec.shape[0]
    cvec = jnp.pad(cvec, ((0, -n_cond % 8), (0, 0)))
    mod = _modulation(cvec, w_mod, b_mod).reshape(depth, -1, 3 * N_SUB, 1, d)

    def mod_vec(i, s, j, which):
        return mod[i, s["rows"], 3 * j + which]

    def pre_args(i, s, j):
        return (mod_vec(i, s, j, 0), mod_vec(i, s, j, 1), g_pre[i, j].reshape(1, d))

    def post_args(i, s, j, o, weight):
        return (o, mod_vec(i, s, j, 2), g_post[i, j].reshape(1, d), weight)

    tables = {s["seq"]: _dft_tables(s["seq"]) for s in streams} if depth > 1 else {}

    pos = _latent_pos_embed(lat["seq"], d)
    _, ctx["u"] = _post_pre(ctx["x"], ctx["seq"], pre=pre_args(0, ctx, 0))
    lat["x"], lat["u"] = _post_pre(lat["x"], lat["seq"], pos=pos, pre=pre_args(0, lat, 0))

    new_states = []
    for i in range(depth):
        kind, li = i % 2, i // 2
        for j in range(N_SUB):
            if j == 1 and kind == 0:
                outs = []
                for s, h0, emit in ((ctx, None, True), (lat, state_ssd[:, li], False)):
                    o, st = _ssd_mixer(s["u"], s["bsz"], s["seq"], h0, emit, ssd_w_in, ssd_w_out16, li,
                                       ssd_conv_w[li], ssd_conv_b[li], ssd_dt_bias[li], ssd_a_log[li],
                                       ssd_d[li], ssd_norm[li])
                    outs.append(o)
                    if emit:
                        new_states.append(st.astype(x_prompt.dtype))
                weight = 1.0
            elif j == 1:
                outs = []
                for s in streams:
                    spectra = _hyena_spectra(s["seq"], d, hy_f_w1[li], hy_f_b1[li], hy_f_w2[li], hy_f_b2[li],
                                             hy_f_w3[li], hy_f_b3[li], hy_f_freq[li], hy_f_w_out[li],
                                             tables[s["seq"]])
                    outs.append(_hyena_mixer(s["u"], s["bsz"], s["seq"], hy_w_in, hy_w_out, li, hy_b_in[li],
                                             hy_short_w[li], hy_short_b[li], spectra, hy_bias[li], hy_b_out[li],
                                             tables[s["seq"]]))
                weight = 1.0
            else:
                slot = (i, j // 2)
                outs = [_matmul(_gate_up(s["u"], ffn_w_gate, ffn_w_up, slot), ffn_w_down16, w_index=slot,
                                out_dtype=BF16, tm=512, tn=512) for s in streams]
                weight = 0.5
            nxt = (i, j + 1) if j + 1 < N_SUB else ((i + 1, 0) if i + 1 < depth else None)
            for s, o in zip(streams, outs):
                pre = pre_args(nxt[0], s, nxt[1]) if nxt is not None else None
                s["x"], s["u"] = _post_pre(s["x"], s["seq"], post=post_args(i, s, j, o, weight), pre=pre)

    y_prompt = ctx["x"].reshape(x_prompt.shape)
    y_sample = lat["x"].reshape(x_sample.shape)
    return (y_prompt, y_sample, jnp.stack(new_states, axis=1))
```

```python
import functools
import math

import jax
import jax.numpy as jnp
from jax import lax
from jax.experimental import pallas as pl
from jax.experimental.pallas import tpu as pltpu

F32 = jnp.float32
BF16 = jnp.bfloat16
HIGHEST = lax.Precision.HIGHEST

RMS_EPS = 1e-6
_LOG2E = math.log2(math.e)
N_SUB = 3
GRID_W = 64

SSD_HEAD_DIM = 64
SSD_GROUPS = 8
D_STATE = 128
SSD_CHUNK = 128

HY_BANDS = 16
HY_DECAY_TARGET = 1e-2
HY_FAST_DECAY = 0.3
HY_SLOW_DECAY = 1.5

_V7X_VMEM_BYTES = 64 * 1024 * 1024
_VMEM_LIMIT = _V7X_VMEM_BYTES - 8 * 1024 * 1024
_LANE = 128


def _params(*semantics):
    return pltpu.CompilerParams(dimension_semantics=semantics, vmem_limit_bytes=_VMEM_LIMIT)


def _tile(dim, target):
    if dim <= target:
        return dim
    for t in range(target - target % _LANE, 0, -_LANE):
        if dim % t == 0:
            return t
    raise ValueError(f"no {_LANE}-aligned tile <= {target} divides {dim}")


def _nt_dot(a, b, **kw):
    return lax.dot_general(a, b, (((1,), (1,)), ((), ())), preferred_element_type=F32, **kw)


def _silu(x):
    h = 0.5 * x
    return h + h * jnp.tanh(h)


def _softplus(x):
    return jnp.maximum(x, 0.0) + jnp.log1p(jnp.exp(-jnp.abs(x)))


def _mod_kernel(c_ref, w_ref, b_ref, o_ref):
    a = _silu(c_ref[...]).astype(BF16)
    o_ref[...] = jnp.dot(a, w_ref[...].astype(BF16), preferred_element_type=F32) + b_ref[...]


def _modulation(cvec, w_mod, b_mod, tn=512):
    depth, d, n = w_mod.shape
    r = cvec.shape[0]
    return pl.pallas_call(
        _mod_kernel,
        out_shape=jax.ShapeDtypeStruct((depth, r, n), F32),
        grid=(depth, n // tn),
        in_specs=[
            pl.BlockSpec((r, d), lambda l, j: (0, 0)),
            pl.BlockSpec((None, d, tn), lambda l, j: (l, 0, j)),
            pl.BlockSpec((None, 1, tn), lambda l, j: (l, 0, j)),
        ],
        out_specs=pl.BlockSpec((None, r, tn), lambda l, j: (l, 0, j)),
        compiler_params=_params("parallel", "parallel"),
        name="modulation",
    )(cvec, w_mod, b_mod.reshape(depth, 1, n))


def _rms(v, g):
    return v * lax.rsqrt(jnp.mean(v * v, axis=-1, keepdims=True) + RMS_EPS) * g


def _post_pre_kernel(*refs, has_pos, has_post, has_pre, weight):
    refs = list(refs)
    x = refs.pop(0)[...]
    if has_pos:
        x = x + refs.pop(0)[...]
    if has_post:
        o_ref, gate_ref, gpost_ref = refs.pop(0), refs.pop(0), refs.pop(0)
        x = x + weight * gate_ref[...] * _rms(o_ref[...].astype(F32), gpost_ref[...])
    if has_pre:
        shift_ref, scale_ref, gpre_ref = refs.pop(0), refs.pop(0), refs.pop(0)
        u = _rms(x, gpre_ref[...]) * (1.0 + scale_ref[...]) + shift_ref[...]
    if has_pos or has_post:
        refs.pop(0)[...] = x
    if has_pre:
        refs.pop(0)[...] = u.astype(BF16)


def _post_pre(x, seq_len, *, pos=None, post=None, pre=None, tm=256):
    t, d = x.shape
    tm = min(tm, seq_len)
    assert seq_len % tm == 0 and t % tm == 0

    def mod_spec(m):
        if m.shape[0] == 1:
            return pl.BlockSpec((None, 1, d), lambda i: (0, 0, 0))
        return pl.BlockSpec((None, 1, d), lambda i: ((i * tm) // seq_len, 0, 0))

    row_spec = pl.BlockSpec((tm, d), lambda i: (i, 0))
    vec_spec = pl.BlockSpec((1, d), lambda i: (0, 0))
    args, in_specs = [x], [row_spec]
    if pos is not None:
        per_seq = seq_len // tm
        args.append(pos)
        in_specs.append(pl.BlockSpec((tm, d), lambda i: (i % per_seq, 0)))
    weight = 0.0
    if post is not None:
        o, gate, g_post, weight = post
        args += [o, gate, g_post]
        in_specs += [row_spec, mod_spec(gate), vec_spec]
    if pre is not None:
        shift, scale, g_pre = pre
        args += [shift, scale, g_pre]
        in_specs += [mod_spec(shift), mod_spec(scale), vec_spec]
    out_shape, out_specs = [], []
    new_x = pos is not None or post is not None
    if new_x:
        out_shape.append(jax.ShapeDtypeStruct((t, d), F32))
        out_specs.append(row_spec)
    if pre is not None:
        out_shape.append(jax.ShapeDtypeStruct((t, d), BF16))
        out_specs.append(row_spec)
    outs = pl.pallas_call(
        functools.partial(_post_pre_kernel, has_pos=pos is not None, has_post=post is not None,
                          has_pre=pre is not None, weight=weight),
        out_shape=out_shape,
        grid=(t // tm,),
        in_specs=in_specs,
        out_specs=out_specs,
        compiler_params=_params("parallel"),
        name="post_pre",
    )(*args)
    outs = list(outs)
    x_new = outs.pop(0) if new_x else None
    u = outs.pop(0) if pre is not None else None
    return x_new, u


def _mm_kernel(*refs, has_bias):
    refs = list(refs)
    x_ref, w_ref = refs.pop(0), refs.pop(0)
    b_ref = refs.pop(0) if has_bias else None
    o_ref = refs.pop(0)
    acc = jnp.dot(x_ref[...], w_ref[...].astype(BF16), preferred_element_type=F32)
    if has_bias:
        acc = acc + b_ref[...]
    o_ref[...] = acc.astype(o_ref.dtype)


def _row_tile_spec(tm, kdim):
    return pl.BlockSpec((tm, kdim), lambda i, j: (i, 0), pipeline_mode=pl.Buffered(1))


def _w_spec(w, w_index, tn, col_block0):
    lead = tuple(w_index)
    assert len(lead) == w.ndim - 2
    return pl.BlockSpec((None,) * len(lead) + (w.shape[-2], tn), lambda i, j: lead + (0, col_block0 + j))


def _matmul(x, w, *, w_index=(), col0=0, n=None, bias=None, out_dtype=F32, tm=1024, tn=1024):
    t, kdim = x.shape
    n = w.shape[-1] - col0 if n is None else n
    assert w.shape[-2] == kdim
    tm, tn = _tile(t, tm), _tile(n, tn)
    if tn % _LANE or col0 % tn:
        w, w_index, col0 = w[tuple(w_index)][:, col0:col0 + n], (), 0
    args = [x, w]
    in_specs = [pl.BlockSpec((tm, kdim), lambda i, j: (i, 0)), _w_spec(w, w_index, tn, col0 // tn)]
    if bias is not None:
        args.append(bias.reshape(1, n).astype(F32))
        in_specs.append(pl.BlockSpec((1, tn), lambda i, j: (0, j)))
    return pl.pallas_call(
        functools.partial(_mm_kernel, has_bias=bias is not None),
        out_shape=jax.ShapeDtypeStruct((t, n), out_dtype),
        grid=(t // tm, n // tn),
        in_specs=in_specs,
        out_specs=pl.BlockSpec((tm, tn), lambda i, j: (i, j)),
        compiler_params=_params("parallel", "parallel"),
        name="matmul",
    )(*args)


def _mm_conv_kernel(*refs, seq, act, has_bias):
    refs = list(refs)
    x_ref, w_ref = refs.pop(0), refs.pop(0)
    b_ref = refs.pop(0) if has_bias else None
    cw_ref, cb_ref, o_ref = refs
    acc = jnp.dot(x_ref[...], w_ref[...].astype(BF16), preferred_element_type=F32)
    if has_bias:
        acc = acc + b_ref[...]
    tm = acc.shape[0]
    k_w = cw_ref.shape[0]
    pos = lax.rem(lax.broadcasted_iota(jnp.int32, (tm, 1), 0), seq)
    y = cb_ref[...] + acc * cw_ref[pl.ds(k_w // 2, 1), :]
    for k in range(k_w):
        d = k - k_w // 2
        if d != 0:
            shifted = pltpu.roll(acc, (-d) % tm, 0)
            inside = (pos + d >= 0) & (pos + d < seq)
            y = y + jnp.where(inside, shifted, 0.0) * cw_ref[pl.ds(k, 1), :]
    o_ref[...] = act(y).astype(o_ref.dtype)


def _matmul_conv(x, seq, w, w_index, col0, conv_w, conv_b, act, bias=None, tn=256, rows=1024):
    t, kdim = x.shape
    k_w, c = conv_w.shape
    tm = seq * max(1, min(rows, t) // seq)
    tn = _tile(c, tn)
    assert t % tm == 0 and col0 % tn == 0 and w.shape[-2] == kdim
    args = [x, w]
    in_specs = [_row_tile_spec(tm, kdim), _w_spec(w, w_index, tn, col0 // tn)]
    col_spec = lambda r: pl.BlockSpec((r, tn), lambda i, j: (0, j))
    if bias is not None:
        args.append(bias.reshape(1, c).astype(F32))
        in_specs.append(col_spec(1))
    args += [conv_w.astype(F32), conv_b.reshape(1, c).astype(F32)]
    in_specs += [col_spec(k_w), col_spec(1)]
    return pl.pallas_call(
        functools.partial(_mm_conv_kernel, seq=seq, act=act, has_bias=bias is not None),
        out_shape=jax.ShapeDtypeStruct((t, c), BF16),
        grid=(t // tm, c // tn),
        in_specs=in_specs,
        out_specs=pl.BlockSpec((tm, tn), lambda i, j: (i, j)),
        compiler_params=_params("parallel", "parallel"),
        name="matmul_conv",
    )(*args)


def _gate_up_kernel(u_ref, wg_ref, wu_ref, o_ref):
    u = u_ref[...]
    g = jnp.dot(u, wg_ref[...].astype(BF16), preferred_element_type=F32)
    v = jnp.dot(u, wu_ref[...].astype(BF16), preferred_element_type=F32)
    o_ref[...] = (_silu(g) * v).astype(o_ref.dtype)


def _gate_up(u, w_gate, w_up, w_index, tm=2048, tn=256):
    t, d = u.shape
    n = w_gate.shape[-1]
    tm, tn = _tile(t, tm), _tile(n, tn)
    return pl.pallas_call(
        _gate_up_kernel,
        out_shape=jax.ShapeDtypeStruct((t, n), BF16),
        grid=(t // tm, n // tn),
        in_specs=[_row_tile_spec(tm, d), _w_spec(w_gate, w_index, tn, 0), _w_spec(w_up, w_index, tn, 0)],
        out_specs=pl.BlockSpec((tm, tn), lambda i, j: (i, j)),
        compiler_params=_params("parallel", "parallel"),
        name="gate_up",
    )(u, w_gate, w_up)


_HY_ROWS = 512


def _ssd_scan_kernel(*refs, has_h0, emit_state, hpg, p_dim):
    refs = list(refs)
    xs_ref, b_ref, c_ref, z_ref, dt_ref, dtb_ref, a_ref, dsk_ref, nw_ref = refs[:9]
    refs = refs[9:]
    h0_ref = refs.pop(0) if has_h0 else None
    y_ref = refs.pop(0)
    st_out_ref = refs.pop(0) if emit_state else None
    ybuf_ref, stf_ref, stb_ref, cs_ref, cst_ref, fac_ref = refs

    q = SSD_CHUNK
    seq, gp = xs_ref.shape
    nc = seq // q
    table_unroll = math.gcd(nc, 4)
    t_i = lax.broadcasted_iota(jnp.int32, (q, q), 0)
    s_i = lax.broadcasted_iota(jnp.int32, (q, q), 1)
    masks = (s_i <= t_i, s_i >= t_i)
    tris = (masks[0].astype(F32), masks[1].astype(F32))
    eye = (lax.broadcasted_iota(jnp.int32, (hpg, hpg), 0)
           == lax.broadcasted_iota(jnp.int32, (hpg, hpg), 1)).astype(F32)
    expand_bf = (lax.broadcasted_iota(jnp.int32, (hpg, gp), 1) // p_dim
                 == lax.broadcasted_iota(jnp.int32, (hpg, gp), 0)).astype(F32).astype(BF16)
    pair_lane = lax.broadcasted_iota(jnp.int32, (q, 2 * p_dim), 1)

    st_refs = (stf_ref, stb_ref)
    both = (0, 1)

    def chunk_rows(c):
        return pl.ds(pl.multiple_of(c * q, q), q)

    def decay_tables(chunks):
        jobs = [(c, d) for c in chunks for d in both]
        hs = [slice(d * hpg, (d + 1) * hpg) for d in both]
        dt = [_softplus(dt_ref[chunk_rows(c), :][:, hs[d]] + dtb_ref[...][:, hs[d]]) for c, d in jobs]
        a = [dt[k] * a_ref[...][:, hs[d]] for k, (c, d) in enumerate(jobs)]
        cs = [jnp.dot(tris[d], a[k], precision=HIGHEST, preferred_element_type=F32) for k, (c, d) in enumerate(jobs)]
        cs_t = [_nt_dot(eye, cs[k], precision=HIGHEST) for k in range(len(jobs))]
        end = [cs[k][q - 1:q, :] if d == 0 else cs[k][0:1, :] for k, (c, d) in enumerate(jobs)]
        for k, (c, d) in enumerate(jobs):
            cs_ref[d, c] = cs[k] * _LOG2E
            cst_ref[d, c] = cs_t[k] * _LOG2E
            dec = jnp.broadcast_to(jnp.exp(end[k]), (8, hpg))
            dec_hi = dec.astype(BF16)
            dec_lo = (dec - dec_hi.astype(F32)).astype(BF16)
            rows3 = jnp.concatenate([dt[k], dt[k] * jnp.exp(end[k] - cs[k]), jnp.exp(cs[k])], axis=0).astype(BF16)
            fac_ref[d, c] = jnp.concatenate([rows3, dec_hi, dec_lo], axis=0)

    def chunk_pair(cidx):
        rows = [chunk_rows(c) for c in cidx]
        cs = [cs_ref[d, cidx[d]] for d in both]
        cs_t = [cst_ref[d, cidx[d]] for d in both]
        fx = [jnp.dot(fac_ref[d, cidx[d]], expand_bf, preferred_element_type=F32) for d in both]
        chunk_decay = [fx[d][3 * q:3 * q + 1] + fx[d][3 * q + 8:3 * q + 9] for d in both]
        bm = [b_ref[rows[d], :] for d in both]
        cm = [c_ref[rows[d], :] for d in both]
        cb = [_nt_dot(cm[d], bm[d]) for d in both]
        xs = [xs_ref[rows[d], :].astype(F32) for d in both]
        xg = [(xs[d] * fx[d][:q]).astype(BF16) for d in both]
        xd = [(xs[d] * fx[d][q:2 * q]).astype(BF16) for d in both]
        st = [st_refs[d][...] for d in both]
        y_off = [jnp.dot(cm[d], st[d].astype(BF16), preferred_element_type=F32) * fx[d][2 * q:3 * q] for d in both]
        b_t = [bm[d].astype(F32).T.astype(BF16) for d in both]
        for d in both:
            st_refs[d][...] = st[d] * chunk_decay[d] + jnp.dot(b_t[d], xd[d], preferred_element_type=F32)
        pieces = ([], [])
        for p in range(hpg // 2):
            for d in both:
                xg_pair = xg[d][:, 2 * p * p_dim:(2 * p + 2) * p_dim]
                outs = []
                for r in (2 * p, 2 * p + 1):
                    seg = cs[d][:, r:r + 1] - cs_t[d][r:r + 1, :]
                    m = (cb[d] * jnp.exp2(jnp.where(masks[d], seg, -jnp.inf))).astype(BF16)
                    outs.append(jnp.dot(m, xg_pair, preferred_element_type=F32))
                pieces[d].append(jnp.where(pair_lane < p_dim, outs[0], outs[1]))
        return [jnp.concatenate(pieces[d], axis=1) + y_off[d] for d in both]

    if has_h0:
        stf_ref[...] = h0_ref[0]
        stb_ref[...] = h0_ref[1]
    else:
        stf_ref[...] = jnp.zeros_like(stf_ref)
        stb_ref[...] = jnp.zeros_like(stb_ref)

    def skip_body(i, carry):
        rows = pl.ds(pl.multiple_of(i * q, q), q)
        ybuf_ref[rows, :] = xs_ref[rows, :].astype(F32) * dsk_ref[...]
        return carry

    lax.fori_loop(0, nc, skip_body, 0)

    def tables_body(i, carry):
        decay_tables([i * table_unroll + k for k in range(table_unroll)])
        return carry

    lax.fori_loop(0, nc // table_unroll, tables_body, 0)

    def gate_norm(rows, y):
        y = y * _silu(z_ref[rows, :].astype(F32))
        y_ref[rows, :] = _rms(y, nw_ref[...]).astype(y_ref.dtype)

    def first_half(i, carry):
        yf, yb = chunk_pair((i, nc - 1 - i))
        ybuf_ref[chunk_rows(i), :] += yf
        ybuf_ref[chunk_rows(nc - 1 - i), :] += yb
        return carry

    def second_half(i, carry):
        yf, yb = chunk_pair((i, nc - 1 - i))
        gate_norm(chunk_rows(i), ybuf_ref[chunk_rows(i), :] + yf)
        gate_norm(chunk_rows(nc - 1 - i), ybuf_ref[chunk_rows(nc - 1 - i), :] + yb)
        return carry

    lax.fori_loop(0, nc // 2, first_half, 0)
    lax.fori_loop(nc // 2, nc, second_half, 0)

    if emit_state:
        st_out_ref[0] = stf_ref[...]
        st_out_ref[1] = stb_ref[...]


def _ssd_scan(z, xbc, dt, dt_bias, a_head, d_skip, norm_w, h0, emit_state, d_inner):
    b, l, _ = z.shape
    g = SSD_GROUPS
    gp = d_inner // g
    n = D_STATE
    hpg = gp // SSD_HEAD_DIM
    assert gp % _LANE == 0 and l % SSD_CHUNK == 0 and n == _LANE
    nc = l // SSD_CHUNK
    assert nc % 2 == 0
    xoff = d_inner // n
    in_specs = [
        pl.BlockSpec((None, l, gp), lambda i, j: (i, 0, j)),
        pl.BlockSpec((None, l, n), lambda i, j: (i, 0, xoff + j)),
        pl.BlockSpec((None, l, n), lambda i, j: (i, 0, xoff + g + j)),
        pl.BlockSpec((None, l, gp), lambda i, j: (i, 0, j)),
        pl.BlockSpec((None, None, l, 2 * hpg), lambda i, j: (i, j, 0, 0)),
        pl.BlockSpec((None, 1, 2 * hpg), lambda i, j: (j, 0, 0)),
        pl.BlockSpec((None, 1, 2 * hpg), lambda i, j: (j, 0, 0)),
        pl.BlockSpec((1, gp), lambda i, j: (0, j)),
        pl.BlockSpec((1, gp), lambda i, j: (0, j)),
    ]
    args = [xbc, xbc, xbc, z, dt, dt_bias, a_head, d_skip, norm_w]
    state_spec = pl.BlockSpec((None, 2, None, n, gp), lambda i, j: (i, 0, j, 0, 0))
    if h0 is not None:
        in_specs.append(state_spec)
        args.append(h0)
    out_shape = [jax.ShapeDtypeStruct((b, l, d_inner), BF16)]
    out_specs = [pl.BlockSpec((None, l, gp), lambda i, j: (i, 0, j))]
    if emit_state:
        out_shape.append(jax.ShapeDtypeStruct((b, 2, g, n, gp), F32))
        out_specs.append(state_spec)
    outs = pl.pallas_call(
        functools.partial(_ssd_scan_kernel, has_h0=h0 is not None, emit_state=emit_state,
                          hpg=hpg, p_dim=SSD_HEAD_DIM),
        out_shape=out_shape,
        grid=(b, g),
        in_specs=in_specs,
        out_specs=out_specs,
        scratch_shapes=[pltpu.VMEM((l, gp), F32), pltpu.VMEM((n, gp), F32), pltpu.VMEM((n, gp), F32),
                        pltpu.VMEM((2, nc, SSD_CHUNK, hpg), F32), pltpu.VMEM((2, nc, hpg, SSD_CHUNK), F32),
                        pltpu.VMEM((2, nc, 3 * SSD_CHUNK + 16, hpg), BF16)],
        compiler_params=_params("parallel", "parallel"),
        name="ssd_scan",
    )(*args)
    return outs[0], (outs[1] if emit_state else None)


def _ssd_mixer(u, bsz, seq, h0, emit_state, w_in, w_out, li, conv_w, conv_b, dt_bias, a_log, d_skip, norm_w):
    d_inner = w_out.shape[1]
    heads = d_inner // SSD_HEAD_DIM
    g = SSD_GROUPS
    hpg = heads // g
    gp = d_inner // g
    n_main = w_in.shape[-1] - 2 * heads
    z = _matmul(u, w_in, w_index=(li,), n=d_inner, out_dtype=BF16, tn=512).reshape(bsz, seq, d_inner)
    xbc = _matmul_conv(u, seq, w_in, (li,), d_inner, conv_w, conv_b, _silu).reshape(bsz, seq, n_main - d_inner)
    dt = _matmul(u, w_in, w_index=(li,), col0=n_main, out_dtype=F32, tn=2 * heads)
    dt = dt.reshape(bsz, seq, 2, g, hpg).transpose(0, 3, 1, 2, 4).reshape(bsz, g, seq, 2 * hpg)
    per_group = lambda v: v.astype(F32).reshape(2, g, hpg).transpose(1, 0, 2).reshape(g, 1, 2 * hpg)
    if h0 is not None:
        h0 = h0.astype(F32).reshape(bsz, 2, g, hpg, SSD_HEAD_DIM, D_STATE)
        h0 = h0.transpose(0, 1, 2, 5, 3, 4).reshape(bsz, 2, g, D_STATE, gp)
    y, st = _ssd_scan(z, xbc, dt, per_group(dt_bias), per_group(-jnp.exp(a_log.astype(F32))),
                      jnp.repeat(d_skip.astype(F32), SSD_HEAD_DIM).reshape(1, d_inner),
                      norm_w.astype(F32).reshape(1, d_inner), h0, emit_state, d_inner)
    out = _matmul(y.reshape(bsz * seq, d_inner), w_out, w_index=(li,), out_dtype=BF16, tm=512, tn=512)
    if st is not None:
        st = st.reshape(bsz, 2, g, D_STATE, hpg, SSD_HEAD_DIM).transpose(0, 1, 2, 4, 5, 3)
        st = st.reshape(bsz, 2, heads, SSD_HEAD_DIM, D_STATE)
    return out, st


def _hy_mlp_kernel(bands_ref, w1t_ref, w1c_ref, w1s_ref, b1_ref, w2_ref, b2_ref, w3_ref, b3_ref, fq_ref,
                   o_ref, *, n):
    idx = lax.broadcasted_iota(jnp.int32, (n, 1), 0).astype(F32)
    hdot = functools.partial(jnp.dot, precision=HIGHEST, preferred_element_type=F32)
    for direction in range(2):
        pos = idx if direction == 0 else (n - 1.0) - idx
        t = pos / (n - 1.0)
        ang = (2.0 * math.pi * pos / n) * bands_ref[...]
        h = t * w1t_ref[...] + hdot(jnp.cos(ang), w1c_ref[...]) - hdot(jnp.sin(ang), w1s_ref[...])
        h = jnp.sin(fq_ref[0:1, :] * (h + b1_ref[...]))
        h = jnp.sin(fq_ref[1:2, :] * (hdot(h, w2_ref[...]) + b2_ref[...]))
        h = jnp.sin(fq_ref[2:3, :] * (hdot(h, w3_ref[...]) + b3_ref[...]))
        o_ref[direction] = h


def _hy_filter_kernel(h_ref, w_ref, delta_ref, o_ref, *, n):
    direction = pl.program_id(0) % 2
    idx = lax.broadcasted_iota(jnp.int32, (n, 1), 0).astype(F32)
    pos = jnp.where(direction == 0, idx, (n - 1.0) - idx)
    t = pos / (n - 1.0)
    k = jnp.dot(h_ref[...], w_ref[...], precision=HIGHEST, preferred_element_type=F32)
    k = k * jnp.exp(-t * delta_ref[...])
    o_ref[...] = k / (jnp.sum(jnp.abs(k), axis=0, keepdims=True) + RMS_EPS)


def _hy_spectrum_kernel(kf_ref, kb_ref, c_ref, s_ref, kp_ref, kq_ref, kn_ref, kf16_ref, kb16_ref, *, n, rchunk):
    dot = functools.partial(jnp.dot, preferred_element_type=F32)
    chunks = [slice(r0, r0 + rchunk) for r0 in range(0, n, rchunk)]
    alt = _alt_sign(0, rchunk)
    sign_n = 1.0 if n % 2 == 0 else -1.0
    nyq = jnp.zeros((1, kf_ref.shape[1]), F32)
    for rs in chunks:
        kf, kb = kf_ref[rs, :], kb_ref[rs, :]
        kf16_ref[rs, :] = kf.astype(BF16)
        kb16_ref[rs, :] = kb.astype(BF16)
        nyq = nyq + jnp.sum(alt * kf, axis=0, keepdims=True) + sign_n * jnp.sum(alt * kb, axis=0, keepdims=True)
    kn_ref[...] = nyq / (2.0 * n)
    for fs in chunks:
        freq = fs.start + lax.broadcasted_iota(jnp.int32, (rchunk, 1), 0)
        wgt = jnp.where(freq == 0, 1.0, 2.0) / (2.0 * n)
        kp_ref[fs, :] = (dot(c_ref[fs, :], kf16_ref[...]) + alt * dot(c_ref[fs, :], kb16_ref[...])) * wgt
        kq_ref[fs, :] = (dot(s_ref[fs, :], kf16_ref[...]) + alt * dot(s_ref[fs, :], kb16_ref[...])) * wgt


def _alt_sign(r0, rows):
    row = lax.broadcasted_iota(jnp.int32, (rows, 1), 0)
    return jnp.where(row % 2 == 0, 1.0, -1.0)


def _hy_conv_kernel(x1_ref, x2_ref, v_ref, kp_ref, kq_ref, kn_ref, bias_ref, c_ref, s_ref, o_ref,
                    z_ref, z16_ref, p2_ref, q2_ref, *, n, rchunk):
    dot = functools.partial(jnp.dot, preferred_element_type=F32)
    chunks = [slice(r0, r0 + rchunk) for r0 in range(0, n, rchunk)]
    alt = _alt_sign(0, rchunk)
    gate_refs = (x1_ref, x2_ref)
    for o in range(2):
        nyq = jnp.zeros((1, z_ref.shape[1]), F32)
        for rs in chunks:
            z = v_ref[rs, :].astype(F32) if o == 0 else z_ref[rs, :]
            if o == 0:
                z_ref[rs, :] = z
            z16_ref[rs, :] = z.astype(BF16)
            nyq = nyq + jnp.sum(alt * z, axis=0, keepdims=True)
        nyq = nyq * kn_ref[o]
        for fs in chunks:
            p = dot(c_ref[fs, :], z16_ref[...])
            q = dot(s_ref[fs, :], z16_ref[...])
            kp, kq = kp_ref[o, fs, :], kq_ref[o, fs, :]
            p2_ref[fs, :] = (p * kp - q * kq).astype(BF16)
            q2_ref[fs, :] = (p * kq + q * kp).astype(BF16)
        for rs in chunks:
            zc = dot(c_ref[rs, :], p2_ref[...]) + dot(s_ref[rs, :], q2_ref[...]) + alt * nyq
            z = gate_refs[o][rs, :].astype(F32) * (zc + z_ref[rs, :] * bias_ref[pl.ds(o, 1), :])
            if o == 0:
                z_ref[rs, :] = z
            else:
                o_ref[rs, :] = z.astype(o_ref.dtype)


def _dft_tables(n):
    f = lax.broadcasted_iota(jnp.int32, (n, n), 0)
    s = lax.broadcasted_iota(jnp.int32, (n, n), 1)
    ang = ((f * s) % (2 * n)).astype(F32) * (math.pi / n)
    return jnp.cos(ang).astype(BF16), jnp.sin(ang).astype(BF16)


def _resident(shape):
    return pl.BlockSpec(shape, lambda *_: (0,) * len(shape), pipeline_mode=pl.Buffered(1))


def _hyena_spectra(n, d, f_w1, f_b1, f_w2, f_b2, f_w3, f_b3, f_freq, f_w_out, tables, tc=512):
    fw = f_w1.shape[1]
    bands = jnp.linspace(1e-4, HY_BANDS - 1, HY_BANDS, dtype=F32).reshape(1, HY_BANDS)
    f32 = lambda a: a.astype(F32)
    row = lambda a: f32(a).reshape(1, -1)
    hdn = pl.pallas_call(
        functools.partial(_hy_mlp_kernel, n=n),
        out_shape=jax.ShapeDtypeStruct((2, n, fw), F32),
        name="hy_mlp",
    )(bands, f32(f_w1[0:1]), f32(f_w1[1:1 + HY_BANDS]), f32(f_w1[1 + HY_BANDS:]), row(f_b1),
      f32(f_w2), row(f_b2), f32(f_w3), row(f_b3), f32(f_freq))
    deltas = jnp.abs(jnp.linspace(math.log(HY_DECAY_TARGET) / HY_SLOW_DECAY,
                                  math.log(HY_DECAY_TARGET) / HY_FAST_DECAY, d, dtype=F32)).reshape(1, d)
    tc = min(tc, d)
    nj = d // tc
    k = pl.pallas_call(
        functools.partial(_hy_filter_kernel, n=n),
        out_shape=jax.ShapeDtypeStruct((4, n, d), F32),
        grid=(4, nj),
        in_specs=[pl.BlockSpec((None, n, fw), lambda i, j: (i % 2, 0, 0)),
                  pl.BlockSpec((fw, tc), lambda i, j: (0, i * nj + j)),
                  pl.BlockSpec((1, tc), lambda i, j: (0, j))],
        out_specs=pl.BlockSpec((None, n, tc), lambda i, j: (i, 0, j)),
        compiler_params=_params("parallel", "parallel"),
        name="hy_filter",
    )(hdn, f32(f_w_out), deltas)
    cos_t, sin_t = tables
    tcs = min(256, d)
    kp, kq, kn = pl.pallas_call(
        functools.partial(_hy_spectrum_kernel, n=n, rchunk=min(_HY_ROWS, n)),
        out_shape=[jax.ShapeDtypeStruct((2, n, d), F32), jax.ShapeDtypeStruct((2, n, d), F32),
                   jax.ShapeDtypeStruct((2, 1, d), F32)],
        grid=(2, d // tcs),
        in_specs=[pl.BlockSpec((None, n, tcs), lambda o, j: (2 * o, 0, j)),
                  pl.BlockSpec((None, n, tcs), lambda o, j: (2 * o + 1, 0, j)),
                  _resident((n, n)), _resident((n, n))],
        out_specs=[pl.BlockSpec((None, n, tcs), lambda o, j: (o, 0, j)),
                   pl.BlockSpec((None, n, tcs), lambda o, j: (o, 0, j)),
                   pl.BlockSpec((None, 1, tcs), lambda o, j: (o, 0, j))],
        scratch_shapes=[pltpu.VMEM((n, tcs), BF16), pltpu.VMEM((n, tcs), BF16)],
        compiler_params=_params("parallel", "parallel"),
        name="hy_spectrum",
    )(k, k, cos_t, sin_t)
    return kp, kq, kn


def _hyena_conv(proj, spectra, bias, tables, tc=256):
    b, n, d3 = proj.shape
    d = d3 // 3
    tc = min(tc, d)
    nj = d // tc
    kp, kq, kn = spectra
    cos_t, sin_t = tables
    part = lambda p: pl.BlockSpec((None, n, tc), lambda j, i: (i, 0, p * nj + j))
    spec = pl.BlockSpec((2, n, tc), lambda j, i: (0, 0, j), pipeline_mode=pl.Buffered(1))
    return pl.pallas_call(
        functools.partial(_hy_conv_kernel, n=n, rchunk=min(_HY_ROWS, n)),
        out_shape=jax.ShapeDtypeStruct((b, n, d), BF16),
        grid=(nj, b),
        in_specs=[part(0), part(1), part(2), spec, spec,
                  pl.BlockSpec((2, 1, tc), lambda j, i: (0, 0, j)),
                  pl.BlockSpec((2, tc), lambda j, i: (0, j)),
                  _resident((n, n)), _resident((n, n))],
        out_specs=pl.BlockSpec((None, n, tc), lambda j, i: (i, 0, j)),
        scratch_shapes=[pltpu.VMEM((n, tc), F32), pltpu.VMEM((n, tc), BF16),
                        pltpu.VMEM((n, tc), BF16), pltpu.VMEM((n, tc), BF16)],
        compiler_params=_params("parallel", "parallel"),
        name="hy_conv",
    )(proj, proj, proj, kp, kq, kn, bias.astype(F32), cos_t, sin_t)


def _hyena_mixer(u, bsz, seq, w_in, w_out, li, b_in, short_w, short_b, spectra, bias, b_out, tables):
    proj = _matmul_conv(u, seq, w_in, (li,), 0, short_w, short_b, lambda y: y, bias=b_in)
    z = _hyena_conv(proj.reshape(bsz, seq, -1), spectra, bias, tables)
    return _matmul(z.reshape(bsz * seq, -1), w_out, w_index=(li,), bias=b_out, out_dtype=BF16, tn=512)


def _latent_pos_embed(n_tok, d):
    rows = n_tok // GRID_W
    r = jnp.repeat(jnp.arange(rows, dtype=F32), GRID_W)
    col = jnp.tile(jnp.arange(GRID_W, dtype=F32), rows)
    quarter = d // 4
    omega = 1.0 / (10000.0 ** (jnp.arange(quarter, dtype=F32) / quarter))
    ar = r[:, None] * omega[None]
    ac = col[:, None] * omega[None]
    return jnp.concatenate([jnp.sin(ar), jnp.cos(ar), jnp.sin(ac), jnp.cos(ac)], axis=-1)


def kernel(x_prompt, x_sample, state_ssd, c, c_ctx, w_mod, b_mod, g_pre, g_post, ffn_w_gate, ffn_w_up, ffn_w_down, ssd_w_in, ssd_conv_w, ssd_conv_b, ssd_dt_bias, ssd_a_log, ssd_d, ssd_norm, ssd_w_out, hy_w_in, hy_b_in, hy_short_w, hy_short_b, hy_f_w1, hy_f_b1, hy_f_w2, hy_f_b2, hy_f_w3, hy_f_b3, hy_f_freq, hy_f_w_out, hy_bias, hy_w_out, hy_b_out):
    depth, d, _ = w_mod.shape
    ffn_w_down16, ssd_w_out16 = ffn_w_down.astype(BF16), ssd_w_out.astype(BF16)

    streams = []
    for x, rows in ((x_prompt, slice(0, 1)), (x_sample, slice(1, 1 + c.shape[0]))):
        bsz, seq, _ = x.shape
        streams.append(dict(bsz=bsz, seq=seq, x=x.reshape(bsz * seq, d), rows=rows))
    ctx, lat = streams

    cvec = jnp.concatenate([c_ctx[None], c], axis=0)
    n_cond = cvec.shape[0]
    cvec = jnp.pad(cvec, ((0, -n_cond % 8), (0, 0)))
    mod = _modulation(cvec, w_mod, b_mod).reshape(depth, -1, 3 * N_SUB, 1, d)

    def mod_vec(i, s, j, which):
        return mod[i, s["rows"], 3 * j + which]

    def pre_args(i, s, j):
        return (mod_vec(i, s, j, 0), mod_vec(i, s, j, 1), g_pre[i, j].reshape(1, d))

    def post_args(i, s, j, o, weight):
        return (o, mod_vec(i, s, j, 2), g_post[i, j].reshape(1, d), weight)

    tables = {s["seq"]: _dft_tables(s["seq"]) for s in streams} if depth > 1 else {}

    pos = _latent_pos_embed(lat["seq"], d)
    _, ctx["u"] = _post_pre(ctx["x"], ctx["seq"], pre=pre_args(0, ctx, 0))
    lat["x"], lat["u"] = _post_pre(lat["x"], lat["seq"], pos=pos, pre=pre_args(0, lat, 0))

    new_states = []
    for i in range(depth):
        kind, li = i % 2, i // 2
        for j in range(N_SUB):
            if j == 1 and kind == 0:
                outs = []
                for s, h0, emit in ((ctx, None, True), (lat, state_ssd[:, li], False)):
                    o, st = _ssd_mixer(s["u"], s["bsz"], s["seq"], h0, emit, ssd_w_in, ssd_w_out16, li,
                                       ssd_conv_w[li], ssd_conv_b[li], ssd_dt_bias[li], ssd_a_log[li],
                                       ssd_d[li], ssd_norm[li])
                    outs.append(o)
                    if emit:
                        new_states.append(st.astype(x_prompt.dtype))
                weight = 1.0
            elif j == 1:
                outs = []
                for s in streams:
                    spectra = _hyena_spectra(s["seq"], d, hy_f_w1[li], hy_f_b1[li], hy_f_w2[li], hy_f_b2[li],
                                             hy_f_w3[li], hy_f_b3[li], hy_f_freq[li], hy_f_w_out[li],
                                             tables[s["seq"]])
                    outs.append(_hyena_mixer(s["u"], s["bsz"], s["seq"], hy_w_in, hy_w_out, li, hy_b_in[li],
                                             hy_short_w[li], hy_short_b[li], spectra, hy_bias[li], hy_b_out[li],
                                             tables[s["seq"]]))
                weight = 1.0
            else:
                slot = (i, j // 2)
                outs = [_matmul(_gate_up(s["u"], ffn_w_gate, ffn_w_up, slot), ffn_w_down16, w_index=slot,
                                out_dtype=BF16, tm=512, tn=512) for s in streams]
                weight = 0.5
            nxt = (i, j + 1) if j + 1 < N_SUB else ((i + 1, 0) if i + 1 < depth else None)
            for s, o in zip(streams, outs):
                pre = pre_args(nxt[0], s, nxt[1]) if nxt is not None else None
                s["x"], s["u"] = _post_pre(s["x"], s["seq"], post=post_args(i, s, j, o, weight), pre=pre)

    y_prompt = ctx["x"].reshape(x_prompt.shape)
    y_sample = lat["x"].reshape(x_sample.shape)
    return (y_prompt, y_sample, jnp.stack(new_states, axis=1))
```

```python
import functools
import math

import jax
import jax.numpy as jnp
from jax import lax
from jax.experimental import pallas as pl
from jax.experimental.pallas import tpu as pltpu

F32 = jnp.float32
BF16 = jnp.bfloat16
HIGHEST = lax.Precision.HIGHEST

RMS_EPS = 1e-6
_LOG2E = math.log2(math.e)
N_SUB = 3
GRID_W = 64

SSD_HEAD_DIM = 64
SSD_GROUPS = 8
D_STATE = 128
SSD_CHUNK = 128

HY_BANDS = 16
HY_DECAY_TARGET = 1e-2
HY_FAST_DECAY = 0.3
HY_SLOW_DECAY = 1.5

_V7X_VMEM_BYTES = 64 * 1024 * 1024
_VMEM_LIMIT = _V7X_VMEM_BYTES - 8 * 1024 * 1024
_LANE = 128


def _params(*semantics):
    return pltpu.CompilerParams(dimension_semantics=semantics, vmem_limit_bytes=_VMEM_LIMIT)


def _tile(dim, target):
    if dim <= target:
        return dim
    for t in range(target - target % _LANE, 0, -_LANE):
        if dim % t == 0:
            return t
    raise ValueError(f"no {_LANE}-aligned tile <= {target} divides {dim}")


def _nt_dot(a, b, **kw):
    return lax.dot_general(a, b, (((1,), (1,)), ((), ())), preferred_element_type=F32, **kw)


def _silu(x):
    h = 0.5 * x
    return h + h * jnp.tanh(h)


def _softplus(x):
    return jnp.maximum(x, 0.0) + jnp.log1p(jnp.exp(-jnp.abs(x)))


def _mod_kernel(c_ref, w_ref, b_ref, o_ref):
    a = _silu(c_ref[...]).astype(BF16)
    o_ref[...] = jnp.dot(a, w_ref[...].astype(BF16), preferred_element_type=F32) + b_ref[...]


def _modulation(cvec, w_mod, b_mod, tn=512):
    depth, d, n = w_mod.shape
    r = cvec.shape[0]
    return pl.pallas_call(
        _mod_kernel,
        out_shape=jax.ShapeDtypeStruct((depth, r, n), F32),
        grid=(depth, n // tn),
        in_specs=[
            pl.BlockSpec((r, d), lambda l, j: (0, 0)),
            pl.BlockSpec((None, d, tn), lambda l, j: (l, 0, j)),
            pl.BlockSpec((None, 1, tn), lambda l, j: (l, 0, j)),
        ],
        out_specs=pl.BlockSpec((None, r, tn), lambda l, j: (l, 0, j)),
        compiler_params=_params("parallel", "parallel"),
        name="modulation",
    )(cvec, w_mod, b_mod.reshape(depth, 1, n))


def _rms(v, g):
    return v * lax.rsqrt(jnp.mean(v * v, axis=-1, keepdims=True) + RMS_EPS) * g


def _post_pre_kernel(*refs, has_pos, has_post, has_pre, weight):
    refs = list(refs)
    x = refs.pop(0)[...]
    if has_pos:
        x = x + refs.pop(0)[...]
    if has_post:
        o_ref, gate_ref, gpost_ref = refs.pop(0), refs.pop(0), refs.pop(0)
        x = x + weight * gate_ref[...] * _rms(o_ref[...].astype(F32), gpost_ref[...])
    if has_pre:
        shift_ref, scale_ref, gpre_ref = refs.pop(0), refs.pop(0), refs.pop(0)
        u = _rms(x, gpre_ref[...]) * (1.0 + scale_ref[...]) + shift_ref[...]
    if has_pos or has_post:
        refs.pop(0)[...] = x
    if has_pre:
        refs.pop(0)[...] = u.astype(BF16)


def _post_pre(x, seq_len, *, pos=None, post=None, pre=None, tm=256):
    t, d = x.shape
    tm = min(tm, seq_len)
    assert seq_len % tm == 0 and t % tm == 0

    def mod_spec(m):
        if m.shape[0] == 1:
            return pl.BlockSpec((None, 1, d), lambda i: (0, 0, 0))
        return pl.BlockSpec((None, 1, d), lambda i: ((i * tm) // seq_len, 0, 0))

    row_spec = pl.BlockSpec((tm, d), lambda i: (i, 0))
    vec_spec = pl.BlockSpec((1, d), lambda i: (0, 0))
    args, in_specs = [x], [row_spec]
    if pos is not None:
        per_seq = seq_len // tm
        args.append(pos)
        in_specs.append(pl.BlockSpec((tm, d), lambda i: (i % per_seq, 0)))
    weight = 0.0
    if post is not None:
        o, gate, g_post, weight = post
        args += [o, gate, g_post]
        in_specs += [row_spec, mod_spec(gate), vec_spec]
    if pre is not None:
        shift, scale, g_pre = pre
        args += [shift, scale, g_pre]
        in_specs += [mod_spec(shift), mod_spec(scale), vec_spec]
    out_shape, out_specs = [], []
    new_x = pos is not None or post is not None
    if new_x:
        out_shape.append(jax.ShapeDtypeStruct((t, d), F32))
        out_specs.append(row_spec)
    if pre is not None:
        out_shape.append(jax.ShapeDtypeStruct((t, d), BF16))
        out_specs.append(row_spec)
    outs = pl.pallas_call(
        functools.partial(_post_pre_kernel, has_pos=pos is not None, has_post=post is not None,
                          has_pre=pre is not None, weight=weight),
        out_shape=out_shape,
        grid=(t // tm,),
        in_specs=in_specs,
        out_specs=out_specs,
        compiler_params=_params("parallel"),
        name="post_pre",
    )(*args)
    outs = list(outs)
    x_new = outs.pop(0) if new_x else None
    u = outs.pop(0) if pre is not None else None
    return x_new, u


def _mm_kernel(*refs, has_bias):
    refs = list(refs)
    x_ref, w_ref = refs.pop(0), refs.pop(0)
    b_ref = refs.pop(0) if has_bias else None
    o_ref = refs.pop(0)
    acc = jnp.dot(x_ref[...], w_ref[...].astype(BF16), preferred_element_type=F32)
    if has_bias:
        acc = acc + b_ref[...]
    o_ref[...] = acc.astype(o_ref.dtype)


def _row_tile_spec(tm, kdim):
    return pl.BlockSpec((tm, kdim), lambda i, j: (i, 0), pipeline_mode=pl.Buffered(1))


def _w_spec(w, w_index, tn, col_block0):
    lead = tuple(w_index)
    assert len(lead) == w.ndim - 2
    return pl.BlockSpec((None,) * len(lead) + (w.shape[-2], tn), lambda i, j: lead + (0, col_block0 + j))


def _matmul(x, w, *, w_index=(), col0=0, n=None, bias=None, out_dtype=F32, tm=1024, tn=1024):
    t, kdim = x.shape
    n = w.shape[-1] - col0 if n is None else n
    assert w.shape[-2] == kdim
    tm, tn = _tile(t, tm), _tile(n, tn)
    if tn % _LANE or col0 % tn:
        w, w_index, col0 = w[tuple(w_index)][:, col0:col0 + n], (), 0
    args = [x, w]
    in_specs = [pl.BlockSpec((tm, kdim), lambda i, j: (i, 0)), _w_spec(w, w_index, tn, col0 // tn)]
    if bias is not None:
        args.append(bias.reshape(1, n).astype(F32))
        in_specs.append(pl.BlockSpec((1, tn), lambda i, j: (0, j)))
    return pl.pallas_call(
        functools.partial(_mm_kernel, has_bias=bias is not None),
        out_shape=jax.ShapeDtypeStruct((t, n), out_dtype),
        grid=(t // tm, n // tn),
        in_specs=in_specs,
        out_specs=pl.BlockSpec((tm, tn), lambda i, j: (i, j)),
        compiler_params=_params("parallel", "parallel"),
        name="matmul",
    )(*args)


def _mm_conv_kernel(*refs, seq, act, has_bias):
    refs = list(refs)
    x_ref, w_ref = refs.pop(0), refs.pop(0)
    b_ref = refs.pop(0) if has_bias else None
    cw_ref, cb_ref, o_ref = refs
    acc = jnp.dot(x_ref[...], w_ref[...].astype(BF16), preferred_element_type=F32)
    if has_bias:
        acc = acc + b_ref[...]
    tm = acc.shape[0]
    k_w = cw_ref.shape[0]
    pos = lax.rem(lax.broadcasted_iota(jnp.int32, (tm, 1), 0), seq)
    y = cb_ref[...] + acc * cw_ref[pl.ds(k_w // 2, 1), :]
    for k in range(k_w):
        d = k - k_w // 2
        if d != 0:
            shifted = pltpu.roll(acc, (-d) % tm, 0)
            inside = (pos + d >= 0) & (pos + d < seq)
            y = y + jnp.where(inside, shifted, 0.0) * cw_ref[pl.ds(k, 1), :]
    o_ref[...] = act(y).astype(o_ref.dtype)


def _matmul_conv(x, seq, w, w_index, col0, conv_w, conv_b, act, bias=None, tn=256, rows=1024):
    t, kdim = x.shape
    k_w, c = conv_w.shape
    tm = seq * max(1, min(rows, t) // seq)
    tn = _tile(c, tn)
    assert t % tm == 0 and col0 % tn == 0 and w.shape[-2] == kdim
    args = [x, w]
    in_specs = [_row_tile_spec(tm, kdim), _w_spec(w, w_index, tn, col0 // tn)]
    col_spec = lambda r: pl.BlockSpec((r, tn), lambda i, j: (0, j))
    if bias is not None:
        args.append(bias.reshape(1, c).astype(F32))
        in_specs.append(col_spec(1))
    args += [conv_w.astype(F32), conv_b.reshape(1, c).astype(F32)]
    in_specs += [col_spec(k_w), col_spec(1)]
    return pl.pallas_call(
        functools.partial(_mm_conv_kernel, seq=seq, act=act, has_bias=bias is not None),
        out_shape=jax.ShapeDtypeStruct((t, c), BF16),
        grid=(t // tm, c // tn),
        in_specs=in_specs,
        out_specs=pl.BlockSpec((tm, tn), lambda i, j: (i, j)),
        compiler_params=_params("parallel", "parallel"),
        name="matmul_conv",
    )(*args)


def _gate_up_kernel(u_ref, wg_ref, wu_ref, o_ref):
    u = u_ref[...]
    g = jnp.dot(u, wg_ref[...].astype(BF16), preferred_element_type=F32)
    v = jnp.dot(u, wu_ref[...].astype(BF16), preferred_element_type=F32)
    o_ref[...] = (_silu(g) * v).astype(o_ref.dtype)


def _gate_up(u, w_gate, w_up, w_index, tm=2048, tn=256):
    t, d = u.shape
    n = w_gate.shape[-1]
    tm, tn = _tile(t, tm), _tile(n, tn)
    return pl.pallas_call(
        _gate_up_kernel,
        out_shape=jax.ShapeDtypeStruct((t, n), BF16),
        grid=(t // tm, n // tn),
        in_specs=[_row_tile_spec(tm, d), _w_spec(w_gate, w_index, tn, 0), _w_spec(w_up, w_index, tn, 0)],
        out_specs=pl.BlockSpec((tm, tn), lambda i, j: (i, j)),
        compiler_params=_params("parallel", "parallel"),
        name="gate_up",
    )(u, w_gate, w_up)


_HY_ROWS = 512


def _ssd_scan_kernel(*refs, has_h0, emit_state, hpg, p_dim):
    refs = list(refs)
    xs_ref, b_ref, c_ref, z_ref, dt_ref, dtb_ref, a_ref, dsk_ref, nw_ref = refs[:9]
    refs = refs[9:]
    h0_ref = refs.pop(0) if has_h0 else None
    y_ref = refs.pop(0)
    st_out_ref = refs.pop(0) if emit_state else None
    ybuf_ref, stf_ref, stb_ref, cs_ref, cst_ref, fac_ref = refs

    q = SSD_CHUNK
    seq, gp = xs_ref.shape
    nc = seq // q
    table_unroll = math.gcd(nc, 4)
    t_i = lax.broadcasted_iota(jnp.int32, (q, q), 0)
    s_i = lax.broadcasted_iota(jnp.int32, (q, q), 1)
    masks = (s_i <= t_i, s_i >= t_i)
    tris = (masks[0].astype(F32), masks[1].astype(F32))
    eye = (lax.broadcasted_iota(jnp.int32, (hpg, hpg), 0)
           == lax.broadcasted_iota(jnp.int32, (hpg, hpg), 1)).astype(F32)
    expand_bf = (lax.broadcasted_iota(jnp.int32, (hpg, gp), 1) // p_dim
                 == lax.broadcasted_iota(jnp.int32, (hpg, gp), 0)).astype(F32).astype(BF16)
    pair_lane = lax.broadcasted_iota(jnp.int32, (q, 2 * p_dim), 1)

    st_refs = (stf_ref, stb_ref)
    both = (0, 1)

    def chunk_rows(c):
        return pl.ds(pl.multiple_of(c * q, q), q)

    def decay_tables(chunks):
        jobs = [(c, d) for c in chunks for d in both]
        hs = [slice(d * hpg, (d + 1) * hpg) for d in both]
        dt = [_softplus(dt_ref[chunk_rows(c), :][:, hs[d]] + dtb_ref[...][:, hs[d]]) for c, d in jobs]
        a = [dt[k] * a_ref[...][:, hs[d]] for k, (c, d) in enumerate(jobs)]
        cs = [jnp.dot(tris[d], a[k], precision=HIGHEST, preferred_element_type=F32) for k, (c, d) in enumerate(jobs)]
        cs_t = [_nt_dot(eye, cs[k], precision=HIGHEST) for k in range(len(jobs))]
        end = [cs[k][q - 1:q, :] if d == 0 else cs[k][0:1, :] for k, (c, d) in enumerate(jobs)]
        for k, (c, d) in enumerate(jobs):
            cs_ref[d, c] = cs[k] * _LOG2E
            cst_ref[d, c] = cs_t[k] * _LOG2E
            dec = jnp.broadcast_to(jnp.exp(end[k]), (8, hpg))
            dec_hi = dec.astype(BF16)
            dec_lo = (dec - dec_hi.astype(F32)).astype(BF16)
            rows3 = jnp.concatenate([dt[k], dt[k] * jnp.exp(end[k] - cs[k]), jnp.exp(cs[k])], axis=0).astype(BF16)
            fac_ref[d, c] = jnp.concatenate([rows3, dec_hi, dec_lo], axis=0)

    def chunk_pair(cidx):
        rows = [chunk_rows(c) for c in cidx]
        cs = [cs_ref[d, cidx[d]] for d in both]
        cs_t = [cst_ref[d, cidx[d]] for d in both]
        fx = [jnp.dot(fac_ref[d, cidx[d]], expand_bf, preferred_element_type=F32) for d in both]
        chunk_decay = [fx[d][3 * q:3 * q + 1] + fx[d][3 * q + 8:3 * q + 9] for d in both]
        bm = [b_ref[rows[d], :] for d in both]
        cm = [c_ref[rows[d], :] for d in both]
        cb = [_nt_dot(cm[d], bm[d]) for d in both]
        xs = [xs_ref[rows[d], :].astype(F32) for d in both]
        xg = [(xs[d] * fx[d][:q]).astype(BF16) for d in both]
        xd = [(xs[d] * fx[d][q:2 * q]).astype(BF16) for d in both]
        st = [st_refs[d][...] for d in both]
        y_off = [jnp.dot(cm[d], st[d].astype(BF16), preferred_element_type=F32) * fx[d][2 * q:3 * q] for d in both]
        b_t = [bm[d].astype(F32).T.astype(BF16) for d in both]
        for d in both:
            st_refs[d][...] = st[d] * chunk_decay[d] + jnp.dot(b_t[d], xd[d], preferred_element_type=F32)
        pieces = ([], [])
        for p in range(hpg // 2):
            for d in both:
                xg_pair = xg[d][:, 2 * p * p_dim:(2 * p + 2) * p_dim]
                outs = []
                for r in (2 * p, 2 * p + 1):
                    seg = cs[d][:, r:r + 1] - cs_t[d][r:r + 1, :]
                    m = (cb[d] * jnp.exp2(jnp.where(masks[d], seg, -jnp.inf))).astype(BF16)
                    outs.append(jnp.dot(m, xg_pair, preferred_element_type=F32))
                pieces[d].append(jnp.where(pair_lane < p_dim, outs[0], outs[1]))
        return [jnp.concatenate(pieces[d], axis=1) + y_off[d] for d in both]

    if has_h0:
        stf_ref[...] = h0_ref[0]
        stb_ref[...] = h0_ref[1]
    else:
        stf_ref[...] = jnp.zeros_like(stf_ref)
        stb_ref[...] = jnp.zeros_like(stb_ref)

    def skip_body(i, carry):
        rows = pl.ds(pl.multiple_of(i * q, q), q)
        ybuf_ref[rows, :] = xs_ref[rows, :].astype(F32) * dsk_ref[...]
        return carry

    lax.fori_loop(0, nc, skip_body, 0)

    def tables_body(i, carry):
        decay_tables([i * table_unroll + k for k in range(table_unroll)])
        return carry

    lax.fori_loop(0, nc // table_unroll, tables_body, 0)

    def gate_norm(rows, y):
        y = y * _silu(z_ref[rows, :].astype(F32))
        y_ref[rows, :] = _rms(y, nw_ref[...]).astype(y_ref.dtype)

    def first_half(i, carry):
        yf, yb = chunk_pair((i, nc - 1 - i))
        ybuf_ref[chunk_rows(i), :] += yf
        ybuf_ref[chunk_rows(nc - 1 - i), :] += yb
        return carry

    def second_half(i, carry):
        yf, yb = chunk_pair((i, nc - 1 - i))
        gate_norm(chunk_rows(i), ybuf_ref[chunk_rows(i), :] + yf)
        gate_norm(chunk_rows(nc - 1 - i), ybuf_ref[chunk_rows(nc - 1 - i), :] + yb)
        return carry

    lax.fori_loop(0, nc // 2, first_half, 0)
    lax.fori_loop(nc // 2, nc, second_half, 0)

    if emit_state:
        st_out_ref[0] = stf_ref[...]
        st_out_ref[1] = stb_ref[...]


def _ssd_scan(z, xbc, dt, dt_bias, a_head, d_skip, norm_w, h0, emit_state, d_inner):
    b, l, _ = z.shape
    g = SSD_GROUPS
    gp = d_inner // g
    n = D_STATE
    hpg = gp // SSD_HEAD_DIM
    assert gp % _LANE == 0 and l % SSD_CHUNK == 0 and n == _LANE
    nc = l // SSD_CHUNK
    assert nc % 2 == 0
    xoff = d_inner // n
    in_specs = [
        pl.BlockSpec((None, l, gp), lambda i, j: (i, 0, j)),
        pl.BlockSpec((None, l, n), lambda i, j: (i, 0, xoff + j)),
        pl.BlockSpec((None, l, n), lambda i, j: (i, 0, xoff + g + j)),
        pl.BlockSpec((None, l, gp), lambda i, j: (i, 0, j)),
        pl.BlockSpec((None, None, l, 2 * hpg), lambda i, j: (i, j, 0, 0)),
        pl.BlockSpec((None, 1, 2 * hpg), lambda i, j: (j, 0, 0)),
        pl.BlockSpec((None, 1, 2 * hpg), lambda i, j: (j, 0, 0)),
        pl.BlockSpec((1, gp), lambda i, j: (0, j)),
        pl.BlockSpec((1, gp), lambda i, j: (0, j)),
    ]
    args = [xbc, xbc, xbc, z, dt, dt_bias, a_head, d_skip, norm_w]
    state_spec = pl.BlockSpec((None, 2, None, n, gp), lambda i, j: (i, 0, j, 0, 0))
    if h0 is not None:
        in_specs.append(state_spec)
        args.append(h0)
    out_shape = [jax.ShapeDtypeStruct((b, l, d_inner), BF16)]
    out_specs = [pl.BlockSpec((None, l, gp), lambda i, j: (i, 0, j))]
    if emit_state:
        out_shape.append(jax.ShapeDtypeStruct((b, 2, g, n, gp), F32))
        out_specs.append(state_spec)
    outs = pl.pallas_call(
        functools.partial(_ssd_scan_kernel, has_h0=h0 is not None, emit_state=emit_state,
                          hpg=hpg, p_dim=SSD_HEAD_DIM),
        out_shape=out_shape,
        grid=(b, g),
        in_specs=in_specs,
        out_specs=out_specs,
        scratch_shapes=[pltpu.VMEM((l, gp), F32), pltpu.VMEM((n, gp), F32), pltpu.VMEM((n, gp), F32),
                        pltpu.VMEM((2, nc, SSD_CHUNK, hpg), F32), pltpu.VMEM((2, nc, hpg, SSD_CHUNK), F32),
                        pltpu.VMEM((2, nc, 3 * SSD_CHUNK + 16, hpg), BF16)],
        compiler_params=_params("parallel", "parallel"),
        name="ssd_scan",
    )(*args)
    return outs[0], (outs[1] if emit_state else None)


def _ssd_mixer(u, bsz, seq, h0, emit_state, w_in, w_out, li, conv_w, conv_b, dt_bias, a_log, d_skip, norm_w):
    d_inner = w_out.shape[1]
    heads = d_inner // SSD_HEAD_DIM
    g = SSD_GROUPS
    hpg = heads // g
    gp = d_inner // g
    n_main = w_in.shape[-1] - 2 * heads
    z = _matmul(u, w_in, w_index=(li,), n=d_inner, out_dtype=BF16, tn=512).reshape(bsz, seq, d_inner)
    xbc = _matmul_conv(u, seq, w_in, (li,), d_inner, conv_w, conv_b, _silu).reshape(bsz, seq, n_main - d_inner)
    dt = _matmul(u, w_in, w_index=(li,), col0=n_main, out_dtype=F32, tn=2 * heads)
    dt = dt.reshape(bsz, seq, 2, g, hpg).transpose(0, 3, 1, 2, 4).reshape(bsz, g, seq, 2 * hpg)
    per_group = lambda v: v.astype(F32).reshape(2, g, hpg).transpose(1, 0, 2).reshape(g, 1, 2 * hpg)
    if h0 is not None:
        h0 = h0.astype(F32).reshape(bsz, 2, g, hpg, SSD_HEAD_DIM, D_STATE)
        h0 = h0.transpose(0, 1, 2, 5, 3, 4).reshape(bsz, 2, g, D_STATE, gp)
    y, st = _ssd_scan(z, xbc, dt, per_group(dt_bias), per_group(-jnp.exp(a_log.astype(F32))),
                      jnp.repeat(d_skip.astype(F32), SSD_HEAD_DIM).reshape(1, d_inner),
                      norm_w.astype(F32).reshape(1, d_inner), h0, emit_state, d_inner)
    out = _matmul(y.reshape(bsz * seq, d_inner), w_out, w_index=(li,), out_dtype=BF16, tm=512, tn=512)
    if st is not None:
        st = st.reshape(bsz, 2, g, D_STATE, hpg, SSD_HEAD_DIM).transpose(0, 1, 2, 4, 5, 3)
        st = st.reshape(bsz, 2, heads, SSD_HEAD_DIM, D_STATE)
    return out, st


def _hy_mlp_kernel(bands_ref, w1t_ref, w1c_ref, w1s_ref, b1_ref, w2_ref, b2_ref, w3_ref, b3_ref, fq_ref,
                   o_ref, *, n):
    idx = lax.broadcasted_iota(jnp.int32, (n, 1), 0).astype(F32)
    hdot = functools.partial(jnp.dot, precision=HIGHEST, preferred_element_type=F32)
    for direction in range(2):
        pos = idx if direction == 0 else (n - 1.0) - idx
        t = pos / (n - 1.0)
        ang = (2.0 * math.pi * pos / n) * bands_ref[...]
        h = t * w1t_ref[...] + hdot(jnp.cos(ang), w1c_ref[...]) - hdot(jnp.sin(ang), w1s_ref[...])
        h = jnp.sin(fq_ref[0:1, :] * (h + b1_ref[...]))
        h = jnp.sin(fq_ref[1:2, :] * (hdot(h, w2_ref[...]) + b2_ref[...]))
        h = jnp.sin(fq_ref[2:3, :] * (hdot(h, w3_ref[...]) + b3_ref[...]))
        o_ref[direction] = h


def _hy_filter_kernel(h_ref, w_ref, delta_ref, o_ref, *, n):
    direction = pl.program_id(0) % 2
    idx = lax.broadcasted_iota(jnp.int32, (n, 1), 0).astype(F32)
    pos = jnp.where(direction == 0, idx, (n - 1.0) - idx)
    t = pos / (n - 1.0)
    k = jnp.dot(h_ref[...], w_ref[...], precision=HIGHEST, preferred_element_type=F32)
    k = k * jnp.exp(-t * delta_ref[...])
    o_ref[...] = k / (jnp.sum(jnp.abs(k), axis=0, keepdims=True) + RMS_EPS)


def _hy_spectrum_kernel(kf_ref, kb_ref, c_ref, s_ref, kp_ref, kq_ref, kn_ref, kf16_ref, kb16_ref, *, n, rchunk):
    dot = functools.partial(jnp.dot, preferred_element_type=F32)
    chunks = [slice(r0, r0 + rchunk) for r0 in range(0, n, rchunk)]
    alt = _alt_sign(0, rchunk)
    sign_n = 1.0 if n % 2 == 0 else -1.0
    nyq = jnp.zeros((1, kf_ref.shape[1]), F32)
    for rs in chunks:
        kf, kb = kf_ref[rs, :], kb_ref[rs, :]
        kf16_ref[rs, :] = kf.astype(BF16)
        kb16_ref[rs, :] = kb.astype(BF16)
        nyq = nyq + jnp.sum(alt * kf, axis=0, keepdims=True) + sign_n * jnp.sum(alt * kb, axis=0, keepdims=True)
    kn_ref[...] = nyq / (2.0 * n)
    for fs in chunks:
        freq = fs.start + lax.broadcasted_iota(jnp.int32, (rchunk, 1), 0)
        wgt = jnp.where(freq == 0, 1.0, 2.0) / (2.0 * n)
        kp_ref[fs, :] = (dot(c_ref[fs, :], kf16_ref[...]) + alt * dot(c_ref[fs, :], kb16_ref[...])) * wgt
        kq_ref[fs, :] = (dot(s_ref[fs, :], kf16_ref[...]) + alt * dot(s_ref[fs, :], kb16_ref[...])) * wgt


def _alt_sign(r0, rows):
    row = lax.broadcasted_iota(jnp.int32, (rows, 1), 0)
    return jnp.where(row % 2 == 0, 1.0, -1.0)


def _hy_conv_kernel(x1_ref, x2_ref, v_ref, kp_ref, kq_ref, kn_ref, bias_ref, c_ref, s_ref, o_ref,
                    z_ref, z16_ref, p2_ref, q2_ref, *, n, rchunk):
    dot = functools.partial(jnp.dot, preferred_element_type=F32)
    chunks = [slice(r0, r0 + rchunk) for r0 in range(0, n, rchunk)]
    alt = _alt_sign(0, rchunk)
    gate_refs = (x1_ref, x2_ref)
    for o in range(2):
        nyq = jnp.zeros((1, z_ref.shape[1]), F32)
        for rs in chunks:
            z = v_ref[rs, :].astype(F32) if o == 0 else z_ref[rs, :]
            if o == 0:
                z_ref[rs, :] = z
            z16_ref[rs, :] = z.astype(BF16)
            nyq = nyq + jnp.sum(alt * z, axis=0, keepdims=True)
        nyq = nyq * kn_ref[o]
        for fs in chunks:
            p = dot(c_ref[fs, :], z16_ref[...])
            q = dot(s_ref[fs, :], z16_ref[...])
            kp, kq = kp_ref[o, fs, :], kq_ref[o, fs, :]
            p2_ref[fs, :] = (p * kp - q * kq).astype(BF16)
            q2_ref[fs, :] = (p * kq + q * kp).astype(BF16)
        for rs in chunks:
            zc = dot(c_ref[rs, :], p2_ref[...]) + dot(s_ref[rs, :], q2_ref[...]) + alt * nyq
            z = gate_refs[o][rs, :].astype(F32) * (zc + z_ref[rs, :] * bias_ref[pl.ds(o, 1), :])
            if o == 0:
                z_ref[rs, :] = z
            else:
                o_ref[rs, :] = z.astype(o_ref.dtype)


def _dft_tables(n):
    f = lax.broadcasted_iota(jnp.int32, (n, n), 0)
    s = lax.broadcasted_iota(jnp.int32, (n, n), 1)
    ang = ((f * s) % (2 * n)).astype(F32) * (math.pi / n)
    return jnp.cos(ang).astype(BF16), jnp.sin(ang).astype(BF16)


def _resident(shape):
    return pl.BlockSpec(shape, lambda *_: (0,) * len(shape), pipeline_mode=pl.Buffered(1))


def _hyena_spectra(n, d, f_w1, f_b1, f_w2, f_b2, f_w3, f_b3, f_freq, f_w_out, tables, tc=512):
    fw = f_w1.shape[1]
    bands = jnp.linspace(1e-4, HY_BANDS - 1, HY_BANDS, dtype=F32).reshape(1, HY_BANDS)
    f32 = lambda a: a.astype(F32)
    row = lambda a: f32(a).reshape(1, -1)
    hdn = pl.pallas_call(
        functools.partial(_hy_mlp_kernel, n=n),
        out_shape=jax.ShapeDtypeStruct((2, n, fw), F32),
        name="hy_mlp",
    )(bands, f32(f_w1[0:1]), f32(f_w1[1:1 + HY_BANDS]), f32(f_w1[1 + HY_BANDS:]), row(f_b1),
      f32(f_w2), row(f_b2), f32(f_w3), row(f_b3), f32(f_freq))
    deltas = jnp.abs(jnp.linspace(math.log(HY_DECAY_TARGET) / HY_SLOW_DECAY,
                                  math.log(HY_DECAY_TARGET) / HY_FAST_DECAY, d, dtype=F32)).reshape(1, d)
    tc = min(tc, d)
    nj = d // tc
    k = pl.pallas_call(
        functools.partial(_hy_filter_kernel, n=n),
        out_shape=jax.ShapeDtypeStruct((4, n, d), F32),
        grid=(4, nj),
        in_specs=[pl.BlockSpec((None, n, fw), lambda i, j: (i % 2, 0, 0)),
                  pl.BlockSpec((fw, tc), lambda i, j: (0, i * nj + j)),
                  pl.BlockSpec((1, tc), lambda i, j: (0, j))],
        out_specs=pl.BlockSpec((None, n, tc), lambda i, j: (i, 0, j)),
        compiler_params=_params("parallel", "parallel"),
        name="hy_filter",
    )(hdn, f32(f_w_out), deltas)
    cos_t, sin_t = tables
    tcs = min(256, d)
    kp, kq, kn = pl.pallas_call(
        functools.partial(_hy_spectrum_kernel, n=n, rchunk=min(_HY_ROWS, n)),
        out_shape=[jax.ShapeDtypeStruct((2, n, d), F32), jax.ShapeDtypeStruct((2, n, d), F32),
                   jax.ShapeDtypeStruct((2, 1, d), F32)],
        grid=(2, d // tcs),
        in_specs=[pl.BlockSpec((None, n, tcs), lambda o, j: (2 * o, 0, j)),
                  pl.BlockSpec((None, n, tcs), lambda o, j: (2 * o + 1, 0, j)),
                  _resident((n, n)), _resident((n, n))],
        out_specs=[pl.BlockSpec((None, n, tcs), lambda o, j: (o, 0, j)),
                   pl.BlockSpec((None, n, tcs), lambda o, j: (o, 0, j)),
                   pl.BlockSpec((None, 1, tcs), lambda o, j: (o, 0, j))],
        scratch_shapes=[pltpu.VMEM((n, tcs), BF16), pltpu.VMEM((n, tcs), BF16)],
        compiler_params=_params("parallel", "parallel"),
        name="hy_spectrum",
    )(k, k, cos_t, sin_t)
    return kp, kq, kn


_HY_BLOCK = 2048 * 256


def _hyena_conv(proj, spectra, bias, tables):
    b, n, d3 = proj.shape
    d = d3 // 3
    tc = _tile(d, max(2 * _LANE, _HY_BLOCK // n))
    nj = d // tc
    kp, kq, kn = spectra
    cos_t, sin_t = tables
    part = lambda p: pl.BlockSpec((None, n, tc), lambda j, i: (i, 0, p * nj + j))
    spec = pl.BlockSpec((2, n, tc), lambda j, i: (0, 0, j), pipeline_mode=pl.Buffered(1))
    return pl.pallas_call(
        functools.partial(_hy_conv_kernel, n=n, rchunk=min(_HY_ROWS, n)),
        out_shape=jax.ShapeDtypeStruct((b, n, d), BF16),
        grid=(nj, b),
        in_specs=[part(0), part(1), part(2), spec, spec,
                  pl.BlockSpec((2, 1, tc), lambda j, i: (0, 0, j)),
                  pl.BlockSpec((2, tc), lambda j, i: (0, j)),
                  _resident((n, n)), _resident((n, n))],
        out_specs=pl.BlockSpec((None, n, tc), lambda j, i: (i, 0, j)),
        scratch_shapes=[pltpu.VMEM((n, tc), F32), pltpu.VMEM((n, tc), BF16),
                        pltpu.VMEM((n, tc), BF16), pltpu.VMEM((n, tc), BF16)],
        compiler_params=_params("parallel", "parallel"),
        name="hy_conv",
    )(proj, proj, proj, kp, kq, kn, bias.astype(F32), cos_t, sin_t)


def _hyena_mixer(u, bsz, seq, w_in, w_out, li, b_in, short_w, short_b, spectra, bias, b_out, tables):
    proj = _matmul_conv(u, seq, w_in, (li,), 0, short_w, short_b, lambda y: y, bias=b_in)
    z = _hyena_conv(proj.reshape(bsz, seq, -1), spectra, bias, tables)
    return _matmul(z.reshape(bsz * seq, -1), w_out, w_index=(li,), bias=b_out, out_dtype=BF16, tn=512)


def _latent_pos_embed(n_tok, d):
    rows = n_tok // GRID_W
    r = jnp.repeat(jnp.arange(rows, dtype=F32), GRID_W)
    col = jnp.tile(jnp.arange(GRID_W, dtype=F32), rows)
    quarter = d // 4
    omega = 1.0 / (10000.0 ** (jnp.arange(quarter, dtype=F32) / quarter))
    ar = r[:, None] * omega[None]
    ac = col[:, None] * omega[None]
    return jnp.concatenate([jnp.sin(ar), jnp.cos(ar), jnp.sin(ac), jnp.cos(ac)], axis=-1)


def kernel(x_prompt, x_sample, state_ssd, c, c_ctx, w_mod, b_mod, g_pre, g_post, ffn_w_gate, ffn_w_up, ffn_w_down, ssd_w_in, ssd_conv_w, ssd_conv_b, ssd_dt_bias, ssd_a_log, ssd_d, ssd_norm, ssd_w_out, hy_w_in, hy_b_in, hy_short_w, hy_short_b, hy_f_w1, hy_f_b1, hy_f_w2, hy_f_b2, hy_f_w3, hy_f_b3, hy_f_freq, hy_f_w_out, hy_bias, hy_w_out, hy_b_out):
    depth, d, _ = w_mod.shape
    ffn_w_down16, ssd_w_out16 = ffn_w_down.astype(BF16), ssd_w_out.astype(BF16)

    streams = []
    for x, rows in ((x_prompt, slice(0, 1)), (x_sample, slice(1, 1 + c.shape[0]))):
        bsz, seq, _ = x.shape
        streams.append(dict(bsz=bsz, seq=seq, x=x.reshape(bsz * seq, d), rows=rows))
    ctx, lat = streams

    cvec = jnp.concatenate([c_ctx[None], c], axis=0)
    n_cond = cvec.shape[0]
    cvec = jnp.pad(cvec, ((0, -n_cond % 8), (0, 0)))
    mod = _modulation(cvec, w_mod, b_mod).reshape(depth, -1, 3 * N_SUB, 1, d)

    def mod_vec(i, s, j, which):
        return mod[i, s["rows"], 3 * j + which]

    def pre_args(i, s, j):
        return (mod_vec(i, s, j, 0), mod_vec(i, s, j, 1), g_pre[i, j].reshape(1, d))

    def post_args(i, s, j, o, weight):
        return (o, mod_vec(i, s, j, 2), g_post[i, j].reshape(1, d), weight)

    tables = {s["seq"]: _dft_tables(s["seq"]) for s in streams} if depth > 1 else {}

    pos = _latent_pos_embed(lat["seq"], d)
    _, ctx["u"] = _post_pre(ctx["x"], ctx["seq"], pre=pre_args(0, ctx, 0))
    lat["x"], lat["u"] = _post_pre(lat["x"], lat["seq"], pos=pos, pre=pre_args(0, lat, 0))

    new_states = []
    for i in range(depth):
        kind, li = i % 2, i // 2
        for j in range(N_SUB):
            if j == 1 and kind == 0:
                outs = []
                for s, h0, emit in ((ctx, None, True), (lat, state_ssd[:, li], False)):
                    o, st = _ssd_mixer(s["u"], s["bsz"], s["seq"], h0, emit, ssd_w_in, ssd_w_out16, li,
                                       ssd_conv_w[li], ssd_conv_b[li], ssd_dt_bias[li], ssd_a_log[li],
                                       ssd_d[li], ssd_norm[li])
                    outs.append(o)
                    if emit:
                        new_states.append(st.astype(x_prompt.dtype))
                weight = 1.0
            elif j == 1:
                outs = []
                for s in streams:
                    spectra = _hyena_spectra(s["seq"], d, hy_f_w1[li], hy_f_b1[li], hy_f_w2[li], hy_f_b2[li],
                                             hy_f_w3[li], hy_f_b3[li], hy_f_freq[li], hy_f_w_out[li],
                                             tables[s["seq"]])
                    outs.append(_hyena_mixer(s["u"], s["bsz"], s["seq"], hy_w_in, hy_w_out, li, hy_b_in[li],
                                             hy_short_w[li], hy_short_b[li], spectra, hy_bias[li], hy_b_out[li],
                                             tables[s["seq"]]))
                weight = 1.0
            else:
                slot = (i, j // 2)
                outs = [_matmul(_gate_up(s["u"], ffn_w_gate, ffn_w_up, slot), ffn_w_down16, w_index=slot,
                                out_dtype=BF16, tm=512, tn=512) for s in streams]
                weight = 0.5
            nxt = (i, j + 1) if j + 1 < N_SUB else ((i + 1, 0) if i + 1 < depth else None)
            for s, o in zip(streams, outs):
                pre = pre_args(nxt[0], s, nxt[1]) if nxt is not None else None
                s["x"], s["u"] = _post_pre(s["x"], s["seq"], post=post_args(i, s, j, o, weight), pre=pre)

    y_prompt = ctx["x"].reshape(x_prompt.shape)
    y_sample = lat["x"].reshape(x_sample.shape)
    return (y_prompt, y_sample, jnp.stack(new_states, axis=1))
```

```python
import functools
import math

import jax
import jax.numpy as jnp
from jax import lax
from jax.experimental import pallas as pl
from jax.experimental.pallas import tpu as pltpu

F32 = jnp.float32
BF16 = jnp.bfloat16
HIGHEST = lax.Precision.HIGHEST

RMS_EPS = 1e-6
_LOG2E = math.log2(math.e)
N_SUB = 3
GRID_W = 64

SSD_HEAD_DIM = 64
SSD_GROUPS = 8
D_STATE = 128
SSD_CHUNK = 128

HY_BANDS = 16
HY_DECAY_TARGET = 1e-2
HY_FAST_DECAY = 0.3
HY_SLOW_DECAY = 1.5

_V7X_VMEM_BYTES = 64 * 1024 * 1024
_VMEM_LIMIT = _V7X_VMEM_BYTES - 8 * 1024 * 1024
_LANE = 128


def _params(*semantics):
    return pltpu.CompilerParams(dimension_semantics=semantics, vmem_limit_bytes=_VMEM_LIMIT)


def _tile(dim, target):
    if dim <= target:
        return dim
    for t in range(target - target % _LANE, 0, -_LANE):
        if dim % t == 0:
            return t
    raise ValueError(f"no {_LANE}-aligned tile <= {target} divides {dim}")


def _nt_dot(a, b, **kw):
    return lax.dot_general(a, b, (((1,), (1,)), ((), ())), preferred_element_type=F32, **kw)


def _silu(x):
    h = 0.5 * x
    return h + h * jnp.tanh(h)


def _softplus(x):
    return jnp.maximum(x, 0.0) + jnp.log1p(jnp.exp(-jnp.abs(x)))


def _mod_kernel(c_ref, w_ref, b_ref, o_ref):
    a = _silu(c_ref[...]).astype(BF16)
    o_ref[...] = jnp.dot(a, w_ref[...].astype(BF16), preferred_element_type=F32) + b_ref[...]


def _modulation(cvec, w_mod, b_mod, tn=512):
    depth, d, n = w_mod.shape
    r = cvec.shape[0]
    return pl.pallas_call(
        _mod_kernel,
        out_shape=jax.ShapeDtypeStruct((depth, r, n), F32),
        grid=(depth, n // tn),
        in_specs=[
            pl.BlockSpec((r, d), lambda l, j: (0, 0)),
            pl.BlockSpec((None, d, tn), lambda l, j: (l, 0, j)),
            pl.BlockSpec((None, 1, tn), lambda l, j: (l, 0, j)),
        ],
        out_specs=pl.BlockSpec((None, r, tn), lambda l, j: (l, 0, j)),
        compiler_params=_params("parallel", "parallel"),
        name="modulation",
    )(cvec, w_mod, b_mod.reshape(depth, 1, n))


def _rms(v, g):
    return v * lax.rsqrt(jnp.mean(v * v, axis=-1, keepdims=True) + RMS_EPS) * g


def _post_pre_kernel(*refs, has_pos, has_post, has_pre, weight):
    refs = list(refs)
    x = refs.pop(0)[...]
    if has_pos:
        x = x + refs.pop(0)[...]
    if has_post:
        o_ref, gate_ref, gpost_ref = refs.pop(0), refs.pop(0), refs.pop(0)
        x = x + weight * gate_ref[...] * _rms(o_ref[...].astype(F32), gpost_ref[...])
    if has_pre:
        shift_ref, scale_ref, gpre_ref = refs.pop(0), refs.pop(0), refs.pop(0)
        u = _rms(x, gpre_ref[...]) * (1.0 + scale_ref[...]) + shift_ref[...]
    if has_pos or has_post:
        refs.pop(0)[...] = x
    if has_pre:
        refs.pop(0)[...] = u.astype(BF16)


def _post_pre(x, seq_len, *, pos=None, post=None, pre=None, tm=256):
    t, d = x.shape
    tm = min(tm, seq_len)
    assert seq_len % tm == 0 and t % tm == 0

    def mod_spec(m):
        if m.shape[0] == 1:
            return pl.BlockSpec((None, 1, d), lambda i: (0, 0, 0))
        return pl.BlockSpec((None, 1, d), lambda i: ((i * tm) // seq_len, 0, 0))

    row_spec = pl.BlockSpec((tm, d), lambda i: (i, 0))
    vec_spec = pl.BlockSpec((1, d), lambda i: (0, 0))
    args, in_specs = [x], [row_spec]
    if pos is not None:
        per_seq = seq_len // tm
        args.append(pos)
        in_specs.append(pl.BlockSpec((tm, d), lambda i: (i % per_seq, 0)))
    weight = 0.0
    if post is not None:
        o, gate, g_post, weight = post
        args += [o, gate, g_post]
        in_specs += [row_spec, mod_spec(gate), vec_spec]
    if pre is not None:
        shift, scale, g_pre = pre
        args += [shift, scale, g_pre]
        in_specs += [mod_spec(shift), mod_spec(scale), vec_spec]
    out_shape, out_specs = [], []
    new_x = pos is not None or post is not None
    if new_x:
        out_shape.append(jax.ShapeDtypeStruct((t, d), F32))
        out_specs.append(row_spec)
    if pre is not None:
        out_shape.append(jax.ShapeDtypeStruct((t, d), BF16))
        out_specs.append(row_spec)
    outs = pl.pallas_call(
        functools.partial(_post_pre_kernel, has_pos=pos is not None, has_post=post is not None,
                          has_pre=pre is not None, weight=weight),
        out_shape=out_shape,
        grid=(t // tm,),
        in_specs=in_specs,
        out_specs=out_specs,
        compiler_params=_params("parallel"),
        name="post_pre",
    )(*args)
    outs = list(outs)
    x_new = outs.pop(0) if new_x else None
    u = outs.pop(0) if pre is not None else None
    return x_new, u


def _mm_kernel(*refs, has_bias):
    refs = list(refs)
    x_ref, w_ref = refs.pop(0), refs.pop(0)
    b_ref = refs.pop(0) if has_bias else None
    o_ref = refs.pop(0)
    acc = jnp.dot(x_ref[...], w_ref[...].astype(BF16), preferred_element_type=F32)
    if has_bias:
        acc = acc + b_ref[...]
    o_ref[...] = acc.astype(o_ref.dtype)


def _row_tile_spec(tm, kdim):
    return pl.BlockSpec((tm, kdim), lambda i, j: (i, 0), pipeline_mode=pl.Buffered(1))


def _w_spec(w, w_index, tn, col_block0):
    lead = tuple(w_index)
    assert len(lead) == w.ndim - 2
    return pl.BlockSpec((None,) * len(lead) + (w.shape[-2], tn), lambda i, j: lead + (0, col_block0 + j))


def _matmul(x, w, *, w_index=(), col0=0, n=None, bias=None, out_dtype=F32, tm=1024, tn=1024):
    t, kdim = x.shape
    n = w.shape[-1] - col0 if n is None else n
    assert w.shape[-2] == kdim
    tm, tn = _tile(t, tm), _tile(n, tn)
    if tn % _LANE or col0 % tn:
        w, w_index, col0 = w[tuple(w_index)][:, col0:col0 + n], (), 0
    args = [x, w]
    in_specs = [pl.BlockSpec((tm, kdim), lambda i, j: (i, 0)), _w_spec(w, w_index, tn, col0 // tn)]
    if bias is not None:
        args.append(bias.reshape(1, n).astype(F32))
        in_specs.append(pl.BlockSpec((1, tn), lambda i, j: (0, j)))
    return pl.pallas_call(
        functools.partial(_mm_kernel, has_bias=bias is not None),
        out_shape=jax.ShapeDtypeStruct((t, n), out_dtype),
        grid=(t // tm, n // tn),
        in_specs=in_specs,
        out_specs=pl.BlockSpec((tm, tn), lambda i, j: (i, j)),
        compiler_params=_params("parallel", "parallel"),
        name="matmul",
    )(*args)


def _mm_conv_kernel(*refs, seq, act, has_bias):
    refs = list(refs)
    x_ref, w_ref = refs.pop(0), refs.pop(0)
    b_ref = refs.pop(0) if has_bias else None
    cw_ref, cb_ref, o_ref = refs
    acc = jnp.dot(x_ref[...], w_ref[...].astype(BF16), preferred_element_type=F32)
    if has_bias:
        acc = acc + b_ref[...]
    tm = acc.shape[0]
    k_w = cw_ref.shape[0]
    pos = lax.rem(lax.broadcasted_iota(jnp.int32, (tm, 1), 0), seq)
    y = cb_ref[...] + acc * cw_ref[pl.ds(k_w // 2, 1), :]
    for k in range(k_w):
        d = k - k_w // 2
        if d != 0:
            shifted = pltpu.roll(acc, (-d) % tm, 0)
            inside = (pos + d >= 0) & (pos + d < seq)
            y = y + jnp.where(inside, shifted, 0.0) * cw_ref[pl.ds(k, 1), :]
    o_ref[...] = act(y).astype(o_ref.dtype)


def _matmul_conv(x, seq, w, w_index, col0, conv_w, conv_b, act, bias=None, tn=256, rows=1024):
    t, kdim = x.shape
    k_w, c = conv_w.shape
    tm = seq * max(1, min(rows, t) // seq)
    tn = _tile(c, tn)
    assert t % tm == 0 and col0 % tn == 0 and w.shape[-2] == kdim
    args = [x, w]
    in_specs = [_row_tile_spec(tm, kdim), _w_spec(w, w_index, tn, col0 // tn)]
    col_spec = lambda r: pl.BlockSpec((r, tn), lambda i, j: (0, j))
    if bias is not None:
        args.append(bias.reshape(1, c).astype(F32))
        in_specs.append(col_spec(1))
    args += [conv_w.astype(F32), conv_b.reshape(1, c).astype(F32)]
    in_specs += [col_spec(k_w), col_spec(1)]
    return pl.pallas_call(
        functools.partial(_mm_conv_kernel, seq=seq, act=act, has_bias=bias is not None),
        out_shape=jax.ShapeDtypeStruct((t, c), BF16),
        grid=(t // tm, c // tn),
        in_specs=in_specs,
        out_specs=pl.BlockSpec((tm, tn), lambda i, j: (i, j)),
        compiler_params=_params("parallel", "parallel"),
        name="matmul_conv",
    )(*args)


def _gate_up_kernel(u_ref, wg_ref, wu_ref, *rest):
    if len(rest) == 3:
        wd_ref, o_ref, wd16_ref = rest
        wd16_ref[...] = wd_ref[...].astype(BF16)
    else:
        o_ref, = rest
    u = u_ref[...]
    g = jnp.dot(u, wg_ref[...].astype(BF16), preferred_element_type=F32)
    v = jnp.dot(u, wu_ref[...].astype(BF16), preferred_element_type=F32)
    o_ref[...] = (_silu(g) * v).astype(o_ref.dtype)


def _gate_up(u, w_gate, w_up, w_index, w_down=None, tm=2048, tn=256):
    t, d = u.shape
    n = w_gate.shape[-1]
    tm, tn = _tile(t, tm), _tile(n, tn)
    nj = n // tn
    args = [u, w_gate, w_up]
    in_specs = [_row_tile_spec(tm, d), _w_spec(w_gate, w_index, tn, 0), _w_spec(w_up, w_index, tn, 0)]
    out_shape = [jax.ShapeDtypeStruct((t, n), BF16)]
    out_specs = [pl.BlockSpec((tm, tn), lambda i, j: (i, j))]
    if w_down is not None:
        steps = (t // tm) * nj
        d_out = w_down.shape[-1]
        slab = next(r for r in range(16, n + 1, 16) if n % r == 0 and n // r <= steps)
        lead = tuple(w_index)
        slab_of = lambda i, j: jnp.minimum(i * nj + j, n // slab - 1)
        args.append(w_down)
        in_specs.append(pl.BlockSpec((None,) * len(lead) + (slab, d_out), lambda i, j: lead + (slab_of(i, j), 0)))
        out_shape.append(jax.ShapeDtypeStruct((n, d_out), BF16))
        out_specs.append(pl.BlockSpec((slab, d_out), lambda i, j: (slab_of(i, j), 0)))
    outs = pl.pallas_call(
        _gate_up_kernel,
        out_shape=out_shape,
        grid=(t // tm, nj),
        in_specs=in_specs,
        out_specs=out_specs,
        compiler_params=_params("arbitrary", "arbitrary") if w_down is not None else _params("parallel", "parallel"),
        name="gate_up",
    )(*args)
    return tuple(outs) if w_down is not None else outs[0]


_HY_ROWS = 512


def _ssd_scan_kernel(*refs, has_h0, emit_state, hpg, p_dim):
    refs = list(refs)
    xs_ref, b_ref, c_ref, z_ref, dt_ref, dtb_ref, a_ref, dsk_ref, nw_ref = refs[:9]
    refs = refs[9:]
    h0_ref = refs.pop(0) if has_h0 else None
    y_ref = refs.pop(0)
    st_out_ref = refs.pop(0) if emit_state else None
    ybuf_ref, stf_ref, stb_ref, cs_ref, cst_ref, fac_ref = refs

    q = SSD_CHUNK
    seq, gp = xs_ref.shape
    nc = seq // q
    table_unroll = math.gcd(nc, 4)
    t_i = lax.broadcasted_iota(jnp.int32, (q, q), 0)
    s_i = lax.broadcasted_iota(jnp.int32, (q, q), 1)
    masks = (s_i <= t_i, s_i >= t_i)
    tris = (masks[0].astype(F32), masks[1].astype(F32))
    eye = (lax.broadcasted_iota(jnp.int32, (hpg, hpg), 0)
           == lax.broadcasted_iota(jnp.int32, (hpg, hpg), 1)).astype(F32)
    expand_bf = (lax.broadcasted_iota(jnp.int32, (hpg, gp), 1) // p_dim
                 == lax.broadcasted_iota(jnp.int32, (hpg, gp), 0)).astype(F32).astype(BF16)
    pair_lane = lax.broadcasted_iota(jnp.int32, (q, 2 * p_dim), 1)

    st_refs = (stf_ref, stb_ref)
    both = (0, 1)

    def chunk_rows(c):
        return pl.ds(pl.multiple_of(c * q, q), q)

    def decay_tables(chunks):
        jobs = [(c, d) for c in chunks for d in both]
        hs = [slice(d * hpg, (d + 1) * hpg) for d in both]
        dt = [_softplus(dt_ref[chunk_rows(c), :][:, hs[d]] + dtb_ref[...][:, hs[d]]) for c, d in jobs]
        a = [dt[k] * a_ref[...][:, hs[d]] for k, (c, d) in enumerate(jobs)]
        cs = [jnp.dot(tris[d], a[k], precision=HIGHEST, preferred_element_type=F32) for k, (c, d) in enumerate(jobs)]
        cs_t = [_nt_dot(eye, cs[k], precision=HIGHEST) for k in range(len(jobs))]
        end = [cs[k][q - 1:q, :] if d == 0 else cs[k][0:1, :] for k, (c, d) in enumerate(jobs)]
        for k, (c, d) in enumerate(jobs):
            cs_ref[d, c] = cs[k] * _LOG2E
            cst_ref[d, c] = cs_t[k] * _LOG2E
            dec = jnp.broadcast_to(jnp.exp(end[k]), (8, hpg))
            dec_hi = dec.astype(BF16)
            dec_lo = (dec - dec_hi.astype(F32)).astype(BF16)
            rows3 = jnp.concatenate([dt[k], dt[k] * jnp.exp(end[k] - cs[k]), jnp.exp(cs[k])], axis=0).astype(BF16)
            fac_ref[d, c] = jnp.concatenate([rows3, dec_hi, dec_lo], axis=0)

    def chunk_pair(cidx):
        rows = [chunk_rows(c) for c in cidx]
        cs = [cs_ref[d, cidx[d]] for d in both]
        cs_t = [cst_ref[d, cidx[d]] for d in both]
        fx = [jnp.dot(fac_ref[d, cidx[d]], expand_bf, preferred_element_type=F32) for d in both]
        chunk_decay = [fx[d][3 * q:3 * q + 1] + fx[d][3 * q + 8:3 * q + 9] for d in both]
        bm = [b_ref[rows[d], :] for d in both]
        cm = [c_ref[rows[d], :] for d in both]
        cb = [_nt_dot(cm[d], bm[d]) for d in both]
        xs = [xs_ref[rows[d], :].astype(F32) for d in both]
        xg = [(xs[d] * fx[d][:q]).astype(BF16) for d in both]
        xd = [(xs[d] * fx[d][q:2 * q]).astype(BF16) for d in both]
        st = [st_refs[d][...] for d in both]
        y_off = [jnp.dot(cm[d], st[d].astype(BF16), preferred_element_type=F32) * fx[d][2 * q:3 * q] for d in both]
        b_t = [bm[d].astype(F32).T.astype(BF16) for d in both]
        for d in both:
            st_refs[d][...] = st[d] * chunk_decay[d] + jnp.dot(b_t[d], xd[d], preferred_element_type=F32)
        pieces = ([], [])
        for p in range(hpg // 2):
            for d in both:
                xg_pair = xg[d][:, 2 * p * p_dim:(2 * p + 2) * p_dim]
                outs = []
                for r in (2 * p, 2 * p + 1):
                    seg = cs[d][:, r:r + 1] - cs_t[d][r:r + 1, :]
                    m = (cb[d] * jnp.exp2(jnp.where(masks[d], seg, -jnp.inf))).astype(BF16)
                    outs.append(jnp.dot(m, xg_pair, preferred_element_type=F32))
                pieces[d].append(jnp.where(pair_lane < p_dim, outs[0], outs[1]))
        return [jnp.concatenate(pieces[d], axis=1) + y_off[d] for d in both]

    if has_h0:
        stf_ref[...] = h0_ref[0]
        stb_ref[...] = h0_ref[1]
    else:
        stf_ref[...] = jnp.zeros_like(stf_ref)
        stb_ref[...] = jnp.zeros_like(stb_ref)

    def skip_body(i, carry):
        rows = pl.ds(pl.multiple_of(i * q, q), q)
        ybuf_ref[rows, :] = xs_ref[rows, :].astype(F32) * dsk_ref[...]
        return carry

    lax.fori_loop(0, nc, skip_body, 0)

    def tables_body(i, carry):
        decay_tables([i * table_unroll + k for k in range(table_unroll)])
        return carry

    lax.fori_loop(0, nc // table_unroll, tables_body, 0)

    def gate_norm(rows, y):
        y = y * _silu(z_ref[rows, :].astype(F32))
        y_ref[rows, :] = _rms(y, nw_ref[...]).astype(y_ref.dtype)

    def first_half(i, carry):
        yf, yb = chunk_pair((i, nc - 1 - i))
        ybuf_ref[chunk_rows(i), :] += yf
        ybuf_ref[chunk_rows(nc - 1 - i), :] += yb
        return carry

    def second_half(i, carry):
        yf, yb = chunk_pair((i, nc - 1 - i))
        gate_norm(chunk_rows(i), ybuf_ref[chunk_rows(i), :] + yf)
        gate_norm(chunk_rows(nc - 1 - i), ybuf_ref[chunk_rows(nc - 1 - i), :] + yb)
        return carry

    lax.fori_loop(0, nc // 2, first_half, 0)
    lax.fori_loop(nc // 2, nc, second_half, 0)

    if emit_state:
        st_out_ref[0] = stf_ref[...]
        st_out_ref[1] = stb_ref[...]


def _ssd_scan(z, xbc, dt, dt_bias, a_head, d_skip, norm_w, h0, emit_state, d_inner):
    b, l, _ = z.shape
    g = SSD_GROUPS
    gp = d_inner // g
    n = D_STATE
    hpg = gp // SSD_HEAD_DIM
    assert gp % _LANE == 0 and l % SSD_CHUNK == 0 and n == _LANE
    nc = l // SSD_CHUNK
    assert nc % 2 == 0
    xoff = d_inner // n
    in_specs = [
        pl.BlockSpec((None, l, gp), lambda i, j: (i, 0, j)),
        pl.BlockSpec((None, l, n), lambda i, j: (i, 0, xoff + j)),
        pl.BlockSpec((None, l, n), lambda i, j: (i, 0, xoff + g + j)),
        pl.BlockSpec((None, l, gp), lambda i, j: (i, 0, j)),
        pl.BlockSpec((None, None, l, 2 * hpg), lambda i, j: (i, j, 0, 0)),
        pl.BlockSpec((None, 1, 2 * hpg), lambda i, j: (j, 0, 0)),
        pl.BlockSpec((None, 1, 2 * hpg), lambda i, j: (j, 0, 0)),
        pl.BlockSpec((1, gp), lambda i, j: (0, j)),
        pl.BlockSpec((1, gp), lambda i, j: (0, j)),
    ]
    args = [xbc, xbc, xbc, z, dt, dt_bias, a_head, d_skip, norm_w]
    state_spec = pl.BlockSpec((None, 2, None, n, gp), lambda i, j: (i, 0, j, 0, 0))
    if h0 is not None:
        in_specs.append(state_spec)
        args.append(h0)
    out_shape = [jax.ShapeDtypeStruct((b, l, d_inner), BF16)]
    out_specs = [pl.BlockSpec((None, l, gp), lambda i, j: (i, 0, j))]
    if emit_state:
        out_shape.append(jax.ShapeDtypeStruct((b, 2, g, n, gp), F32))
        out_specs.append(state_spec)
    outs = pl.pallas_call(
        functools.partial(_ssd_scan_kernel, has_h0=h0 is not None, emit_state=emit_state,
                          hpg=hpg, p_dim=SSD_HEAD_DIM),
        out_shape=out_shape,
        grid=(b, g),
        in_specs=in_specs,
        out_specs=out_specs,
        scratch_shapes=[pltpu.VMEM((l, gp), F32), pltpu.VMEM((n, gp), F32), pltpu.VMEM((n, gp), F32),
                        pltpu.VMEM((2, nc, SSD_CHUNK, hpg), F32), pltpu.VMEM((2, nc, hpg, SSD_CHUNK), F32),
                        pltpu.VMEM((2, nc, 3 * SSD_CHUNK + 16, hpg), BF16)],
        compiler_params=_params("parallel", "parallel"),
        name="ssd_scan",
    )(*args)
    return outs[0], (outs[1] if emit_state else None)


def _ssd_mixer(u, bsz, seq, h0, emit_state, w_in, w_out, li, conv_w, conv_b, dt_bias, a_log, d_skip, norm_w):
    d_inner = w_out.shape[1]
    heads = d_inner // SSD_HEAD_DIM
    g = SSD_GROUPS
    hpg = heads // g
    gp = d_inner // g
    n_main = w_in.shape[-1] - 2 * heads
    z = _matmul(u, w_in, w_index=(li,), n=d_inner, out_dtype=BF16, tn=512).reshape(bsz, seq, d_inner)
    xbc = _matmul_conv(u, seq, w_in, (li,), d_inner, conv_w, conv_b, _silu).reshape(bsz, seq, n_main - d_inner)
    dt = _matmul(u, w_in, w_index=(li,), col0=n_main, out_dtype=F32, tn=2 * heads)
    dt = dt.reshape(bsz, seq, 2, g, hpg).transpose(0, 3, 1, 2, 4).reshape(bsz, g, seq, 2 * hpg)
    per_group = lambda v: v.astype(F32).reshape(2, g, hpg).transpose(1, 0, 2).reshape(g, 1, 2 * hpg)
    if h0 is not None:
        h0 = h0.astype(F32).reshape(bsz, 2, g, hpg, SSD_HEAD_DIM, D_STATE)
        h0 = h0.transpose(0, 1, 2, 5, 3, 4).reshape(bsz, 2, g, D_STATE, gp)
    y, st = _ssd_scan(z, xbc, dt, per_group(dt_bias), per_group(-jnp.exp(a_log.astype(F32))),
                      jnp.repeat(d_skip.astype(F32), SSD_HEAD_DIM).reshape(1, d_inner),
                      norm_w.astype(F32).reshape(1, d_inner), h0, emit_state, d_inner)
    out = _matmul(y.reshape(bsz * seq, d_inner), w_out, w_index=(li,), out_dtype=BF16, tm=512, tn=512)
    if st is not None:
        st = st.reshape(bsz, 2, g, D_STATE, hpg, SSD_HEAD_DIM).transpose(0, 1, 2, 4, 5, 3)
        st = st.reshape(bsz, 2, heads, SSD_HEAD_DIM, D_STATE)
    return out, st


def _hy_mlp_kernel(bands_ref, w1t_ref, w1c_ref, w1s_ref, b1_ref, w2_ref, b2_ref, w3_ref, b3_ref, fq_ref,
                   o_ref, *, n):
    idx = lax.broadcasted_iota(jnp.int32, (n, 1), 0).astype(F32)
    hdot = functools.partial(jnp.dot, precision=HIGHEST, preferred_element_type=F32)
    for direction in range(2):
        pos = idx if direction == 0 else (n - 1.0) - idx
        t = pos / (n - 1.0)
        ang = (2.0 * math.pi * pos / n) * bands_ref[...]
        h = t * w1t_ref[...] + hdot(jnp.cos(ang), w1c_ref[...]) - hdot(jnp.sin(ang), w1s_ref[...])
        h = jnp.sin(fq_ref[0:1, :] * (h + b1_ref[...]))
        h = jnp.sin(fq_ref[1:2, :] * (hdot(h, w2_ref[...]) + b2_ref[...]))
        h = jnp.sin(fq_ref[2:3, :] * (hdot(h, w3_ref[...]) + b3_ref[...]))
        o_ref[direction] = h


def _hy_filter_kernel(h_ref, w_ref, delta_ref, o_ref, *, n):
    direction = pl.program_id(0) % 2
    idx = lax.broadcasted_iota(jnp.int32, (n, 1), 0).astype(F32)
    pos = jnp.where(direction == 0, idx, (n - 1.0) - idx)
    t = pos / (n - 1.0)
    k = jnp.dot(h_ref[...], w_ref[...], precision=HIGHEST, preferred_element_type=F32)
    k = k * jnp.exp(-t * delta_ref[...])
    o_ref[...] = k / (jnp.sum(jnp.abs(k), axis=0, keepdims=True) + RMS_EPS)


def _hy_spectrum_kernel(kf_ref, kb_ref, c_ref, s_ref, kp_ref, kq_ref, kn_ref, kf16_ref, kb16_ref, *, n, rchunk):
    dot = functools.partial(jnp.dot, preferred_element_type=F32)
    chunks = [slice(r0, r0 + rchunk) for r0 in range(0, n, rchunk)]
    alt = _alt_sign(0, rchunk)
    sign_n = 1.0 if n % 2 == 0 else -1.0
    nyq = jnp.zeros((1, kf_ref.shape[1]), F32)
    for rs in chunks:
        kf, kb = kf_ref[rs, :], kb_ref[rs, :]
        kf16_ref[rs, :] = kf.astype(BF16)
        kb16_ref[rs, :] = kb.astype(BF16)
        nyq = nyq + jnp.sum(alt * kf, axis=0, keepdims=True) + sign_n * jnp.sum(alt * kb, axis=0, keepdims=True)
    kn_ref[...] = nyq / (2.0 * n)
    for fs in chunks:
        freq = fs.start + lax.broadcasted_iota(jnp.int32, (rchunk, 1), 0)
        wgt = jnp.where(freq == 0, 1.0, 2.0) / (2.0 * n)
        kp_ref[fs, :] = (dot(c_ref[fs, :], kf16_ref[...]) + alt * dot(c_ref[fs, :], kb16_ref[...])) * wgt
        kq_ref[fs, :] = (dot(s_ref[fs, :], kf16_ref[...]) + alt * dot(s_ref[fs, :], kb16_ref[...])) * wgt


def _alt_sign(r0, rows):
    row = lax.broadcasted_iota(jnp.int32, (rows, 1), 0)
    return jnp.where(row % 2 == 0, 1.0, -1.0)


def _hy_conv_kernel(x1_ref, x2_ref, v_ref, kp_ref, kq_ref, kn_ref, bias_ref, c_ref, s_ref, o_ref,
                    z_ref, z16_ref, p2_ref, q2_ref, *, n, rchunk):
    dot = functools.partial(jnp.dot, preferred_element_type=F32)
    chunks = [slice(r0, r0 + rchunk) for r0 in range(0, n, rchunk)]
    alt = _alt_sign(0, rchunk)
    gate_refs = (x1_ref, x2_ref)
    for o in range(2):
        nyq = jnp.zeros((1, z_ref.shape[1]), F32)
        for rs in chunks:
            z = v_ref[rs, :].astype(F32) if o == 0 else z_ref[rs, :]
            if o == 0:
                z_ref[rs, :] = z
            z16_ref[rs, :] = z.astype(BF16)
            nyq = nyq + jnp.sum(alt * z, axis=0, keepdims=True)
        nyq = nyq * kn_ref[o]
        for fs in chunks:
            p = dot(c_ref[fs, :], z16_ref[...])
            q = dot(s_ref[fs, :], z16_ref[...])
            kp, kq = kp_ref[o, fs, :], kq_ref[o, fs, :]
            p2_ref[fs, :] = (p * kp - q * kq).astype(BF16)
            q2_ref[fs, :] = (p * kq + q * kp).astype(BF16)
        for rs in chunks:
            zc = dot(c_ref[rs, :], p2_ref[...]) + dot(s_ref[rs, :], q2_ref[...]) + alt * nyq
            z = gate_refs[o][rs, :].astype(F32) * (zc + z_ref[rs, :] * bias_ref[pl.ds(o, 1), :])
            if o == 0:
                z_ref[rs, :] = z
            else:
                o_ref[rs, :] = z.astype(o_ref.dtype)


def _dft_tables(n):
    f = lax.broadcasted_iota(jnp.int32, (n, n), 0)
    s = lax.broadcasted_iota(jnp.int32, (n, n), 1)
    ang = ((f * s) % (2 * n)).astype(F32) * (math.pi / n)
    return jnp.cos(ang).astype(BF16), jnp.sin(ang).astype(BF16)


def _resident(shape):
    return pl.BlockSpec(shape, lambda *_: (0,) * len(shape), pipeline_mode=pl.Buffered(1))


def _hyena_spectra(n, d, f_w1, f_b1, f_w2, f_b2, f_w3, f_b3, f_freq, f_w_out, tables, tc=512):
    fw = f_w1.shape[1]
    bands = jnp.linspace(1e-4, HY_BANDS - 1, HY_BANDS, dtype=F32).reshape(1, HY_BANDS)
    f32 = lambda a: a.astype(F32)
    row = lambda a: f32(a).reshape(1, -1)
    hdn = pl.pallas_call(
        functools.partial(_hy_mlp_kernel, n=n),
        out_shape=jax.ShapeDtypeStruct((2, n, fw), F32),
        name="hy_mlp",
    )(bands, f32(f_w1[0:1]), f32(f_w1[1:1 + HY_BANDS]), f32(f_w1[1 + HY_BANDS:]), row(f_b1),
      f32(f_w2), row(f_b2), f32(f_w3), row(f_b3), f32(f_freq))
    deltas = jnp.abs(jnp.linspace(math.log(HY_DECAY_TARGET) / HY_SLOW_DECAY,
                                  math.log(HY_DECAY_TARGET) / HY_FAST_DECAY, d, dtype=F32)).reshape(1, d)
    tc = min(tc, d)
    nj = d // tc
    k = pl.pallas_call(
        functools.partial(_hy_filter_kernel, n=n),
        out_shape=jax.ShapeDtypeStruct((4, n, d), F32),
        grid=(4, nj),
        in_specs=[pl.BlockSpec((None, n, fw), lambda i, j: (i % 2, 0, 0)),
                  pl.BlockSpec((fw, tc), lambda i, j: (0, i * nj + j)),
                  pl.BlockSpec((1, tc), lambda i, j: (0, j))],
        out_specs=pl.BlockSpec((None, n, tc), lambda i, j: (i, 0, j)),
        compiler_params=_params("parallel", "parallel"),
        name="hy_filter",
    )(hdn, f32(f_w_out), deltas)
    cos_t, sin_t = tables
    tcs = min(256, d)
    kp, kq, kn = pl.pallas_call(
        functools.partial(_hy_spectrum_kernel, n=n, rchunk=min(_HY_ROWS, n)),
        out_shape=[jax.ShapeDtypeStruct((2, n, d), F32), jax.ShapeDtypeStruct((2, n, d), F32),
                   jax.ShapeDtypeStruct((2, 1, d), F32)],
        grid=(2, d // tcs),
        in_specs=[pl.BlockSpec((None, n, tcs), lambda o, j: (2 * o, 0, j)),
                  pl.BlockSpec((None, n, tcs), lambda o, j: (2 * o + 1, 0, j)),
                  _resident((n, n)), _resident((n, n))],
        out_specs=[pl.BlockSpec((None, n, tcs), lambda o, j: (o, 0, j)),
                   pl.BlockSpec((None, n, tcs), lambda o, j: (o, 0, j)),
                   pl.BlockSpec((None, 1, tcs), lambda o, j: (o, 0, j))],
        scratch_shapes=[pltpu.VMEM((n, tcs), BF16), pltpu.VMEM((n, tcs), BF16)],
        compiler_params=_params("parallel", "parallel"),
        name="hy_spectrum",
    )(k, k, cos_t, sin_t)
    return kp, kq, kn


_HY_BLOCK = 2048 * 256


def _hyena_conv(proj, spectra, bias, tables):
    b, n, d3 = proj.shape
    d = d3 // 3
    tc = _tile(d, max(2 * _LANE, _HY_BLOCK // n))
    nj = d // tc
    kp, kq, kn = spectra
    cos_t, sin_t = tables
    part = lambda p: pl.BlockSpec((None, n, tc), lambda j, i: (i, 0, p * nj + j))
    spec = pl.BlockSpec((2, n, tc), lambda j, i: (0, 0, j), pipeline_mode=pl.Buffered(1))
    return pl.pallas_call(
        functools.partial(_hy_conv_kernel, n=n, rchunk=min(_HY_ROWS, n)),
        out_shape=jax.ShapeDtypeStruct((b, n, d), BF16),
        grid=(nj, b),
        in_specs=[part(0), part(1), part(2), spec, spec,
                  pl.BlockSpec((2, 1, tc), lambda j, i: (0, 0, j)),
                  pl.BlockSpec((2, tc), lambda j, i: (0, j)),
                  _resident((n, n)), _resident((n, n))],
        out_specs=pl.BlockSpec((None, n, tc), lambda j, i: (i, 0, j)),
        scratch_shapes=[pltpu.VMEM((n, tc), F32), pltpu.VMEM((n, tc), BF16),
                        pltpu.VMEM((n, tc), BF16), pltpu.VMEM((n, tc), BF16)],
        compiler_params=_params("parallel", "parallel"),
        name="hy_conv",
    )(proj, proj, proj, kp, kq, kn, bias.astype(F32), cos_t, sin_t)


def _hyena_mixer(u, bsz, seq, w_in, w_out, li, b_in, short_w, short_b, spectra, bias, b_out, tables):
    proj = _matmul_conv(u, seq, w_in, (li,), 0, short_w, short_b, lambda y: y, bias=b_in)
    z = _hyena_conv(proj.reshape(bsz, seq, -1), spectra, bias, tables)
    return _matmul(z.reshape(bsz * seq, -1), w_out, w_index=(li,), bias=b_out, out_dtype=BF16, tn=512)


def _latent_pos_embed(n_tok, d):
    rows = n_tok // GRID_W
    r = jnp.repeat(jnp.arange(rows, dtype=F32), GRID_W)
    col = jnp.tile(jnp.arange(GRID_W, dtype=F32), rows)
    quarter = d // 4
    omega = 1.0 / (10000.0 ** (jnp.arange(quarter, dtype=F32) / quarter))
    ar = r[:, None] * omega[None]
    ac = col[:, None] * omega[None]
    return jnp.concatenate([jnp.sin(ar), jnp.cos(ar), jnp.sin(ac), jnp.cos(ac)], axis=-1)


def kernel(x_prompt, x_sample, state_ssd, c, c_ctx, w_mod, b_mod, g_pre, g_post, ffn_w_gate, ffn_w_up, ffn_w_down, ssd_w_in, ssd_conv_w, ssd_conv_b, ssd_dt_bias, ssd_a_log, ssd_d, ssd_norm, ssd_w_out, hy_w_in, hy_b_in, hy_short_w, hy_short_b, hy_f_w1, hy_f_b1, hy_f_w2, hy_f_b2, hy_f_w3, hy_f_b3, hy_f_freq, hy_f_w_out, hy_bias, hy_w_out, hy_b_out):
    depth, d, _ = w_mod.shape
    ssd_w_out16 = ssd_w_out.astype(BF16)

    streams = []
    for x, rows in ((x_prompt, slice(0, 1)), (x_sample, slice(1, 1 + c.shape[0]))):
        bsz, seq, _ = x.shape
        streams.append(dict(bsz=bsz, seq=seq, x=x.reshape(bsz * seq, d), rows=rows))
    ctx, lat = streams

    cvec = jnp.concatenate([c_ctx[None], c], axis=0)
    n_cond = cvec.shape[0]
    cvec = jnp.pad(cvec, ((0, -n_cond % 8), (0, 0)))
    mod = _modulation(cvec, w_mod, b_mod).reshape(depth, -1, 3 * N_SUB, 1, d)

    def mod_vec(i, s, j, which):
        return mod[i, s["rows"], 3 * j + which]

    def pre_args(i, s, j):
        return (mod_vec(i, s, j, 0), mod_vec(i, s, j, 1), g_pre[i, j].reshape(1, d))

    def post_args(i, s, j, o, weight):
        return (o, mod_vec(i, s, j, 2), g_post[i, j].reshape(1, d), weight)

    tables = {s["seq"]: _dft_tables(s["seq"]) for s in streams} if depth > 1 else {}

    pos = _latent_pos_embed(lat["seq"], d)
    _, ctx["u"] = _post_pre(ctx["x"], ctx["seq"], pre=pre_args(0, ctx, 0))
    lat["x"], lat["u"] = _post_pre(lat["x"], lat["seq"], pos=pos, pre=pre_args(0, lat, 0))

    new_states = []
    for i in range(depth):
        kind, li = i % 2, i // 2
        for j in range(N_SUB):
            if j == 1 and kind == 0:
                outs = []
                for s, h0, emit in ((ctx, None, True), (lat, state_ssd[:, li], False)):
                    o, st = _ssd_mixer(s["u"], s["bsz"], s["seq"], h0, emit, ssd_w_in, ssd_w_out16, li,
                                       ssd_conv_w[li], ssd_conv_b[li], ssd_dt_bias[li], ssd_a_log[li],
                                       ssd_d[li], ssd_norm[li])
                    outs.append(o)
                    if emit:
                        new_states.append(st.astype(x_prompt.dtype))
                weight = 1.0
            elif j == 1:
                outs = []
                for s in streams:
                    spectra = _hyena_spectra(s["seq"], d, hy_f_w1[li], hy_f_b1[li], hy_f_w2[li], hy_f_b2[li],
                                             hy_f_w3[li], hy_f_b3[li], hy_f_freq[li], hy_f_w_out[li],
                                             tables[s["seq"]])
                    outs.append(_hyena_mixer(s["u"], s["bsz"], s["seq"], hy_w_in, hy_w_out, li, hy_b_in[li],
                                             hy_short_w[li], hy_short_b[li], spectra, hy_bias[li], hy_b_out[li],
                                             tables[s["seq"]]))
                weight = 1.0
            else:
                slot = (i, j // 2)
                h_ctx, w_down16 = _gate_up(ctx["u"], ffn_w_gate, ffn_w_up, slot, w_down=ffn_w_down)
                hidden = (h_ctx, _gate_up(lat["u"], ffn_w_gate, ffn_w_up, slot))
                outs = [_matmul(h, w_down16, out_dtype=BF16, tm=512, tn=512) for h in hidden]
                weight = 0.5
            nxt = (i, j + 1) if j + 1 < N_SUB else ((i + 1, 0) if i + 1 < depth else None)
            for s, o in zip(streams, outs):
                pre = pre_args(nxt[0], s, nxt[1]) if nxt is not None else None
                s["x"], s["u"] = _post_pre(s["x"], s["seq"], post=post_args(i, s, j, o, weight), pre=pre)

    y_prompt = ctx["x"].reshape(x_prompt.shape)
    y_sample = lat["x"].reshape(x_sample.shape)
    return (y_prompt, y_sample, jnp.stack(new_states, axis=1))
```

```python
import functools
import math

import jax
import jax.numpy as jnp
from jax import lax
from jax.experimental import pallas as pl
from jax.experimental.pallas import tpu as pltpu

F32 = jnp.float32
BF16 = jnp.bfloat16
HIGHEST = lax.Precision.HIGHEST

RMS_EPS = 1e-6
_LOG2E = math.log2(math.e)
N_SUB = 3
GRID_W = 64

SSD_HEAD_DIM = 64
SSD_GROUPS = 8
D_STATE = 128
SSD_CHUNK = 128

HY_BANDS = 16
HY_DECAY_TARGET = 1e-2
HY_FAST_DECAY = 0.3
HY_SLOW_DECAY = 1.5

_V7X_VMEM_BYTES = 64 * 1024 * 1024
_VMEM_LIMIT = _V7X_VMEM_BYTES - 8 * 1024 * 1024
_LANE = 128


def _params(*semantics):
    return pltpu.CompilerParams(dimension_semantics=semantics, vmem_limit_bytes=_VMEM_LIMIT)


def _tile(dim, target):
    if dim <= target:
        return dim
    for t in range(target - target % _LANE, 0, -_LANE):
        if dim % t == 0:
            return t
    raise ValueError(f"no {_LANE}-aligned tile <= {target} divides {dim}")


def _nt_dot(a, b, **kw):
    return lax.dot_general(a, b, (((1,), (1,)), ((), ())), preferred_element_type=F32, **kw)


def _silu(x):
    h = 0.5 * x
    return h + h * jnp.tanh(h)


def _softplus(x):
    return jnp.maximum(x, 0.0) + jnp.log1p(jnp.exp(-jnp.abs(x)))


def _mod_kernel(c_ref, w_ref, b_ref, o_ref):
    a = _silu(c_ref[...]).astype(BF16)
    o_ref[...] = jnp.dot(a, w_ref[...].astype(BF16), preferred_element_type=F32) + b_ref[...]


def _modulation(cvec, w_mod, b_mod, tn=512):
    depth, d, n = w_mod.shape
    r = cvec.shape[0]
    return pl.pallas_call(
        _mod_kernel,
        out_shape=jax.ShapeDtypeStruct((depth, r, n), F32),
        grid=(depth, n // tn),
        in_specs=[
            pl.BlockSpec((r, d), lambda l, j: (0, 0)),
            pl.BlockSpec((None, d, tn), lambda l, j: (l, 0, j)),
            pl.BlockSpec((None, 1, tn), lambda l, j: (l, 0, j)),
        ],
        out_specs=pl.BlockSpec((None, r, tn), lambda l, j: (l, 0, j)),
        compiler_params=_params("parallel", "parallel"),
        name="modulation",
    )(cvec, w_mod, b_mod.reshape(depth, 1, n))


def _rms(v, g):
    return v * lax.rsqrt(jnp.mean(v * v, axis=-1, keepdims=True) + RMS_EPS) * g


def _post_pre_kernel(*refs, has_pos, has_post, has_pre, weight):
    refs = list(refs)
    x = refs.pop(0)[...]
    if has_pos:
        x = x + refs.pop(0)[...]
    if has_post:
        o_ref, gate_ref, gpost_ref = refs.pop(0), refs.pop(0), refs.pop(0)
        x = x + weight * gate_ref[...] * _rms(o_ref[...].astype(F32), gpost_ref[...])
    if has_pre:
        shift_ref, scale_ref, gpre_ref = refs.pop(0), refs.pop(0), refs.pop(0)
        u = _rms(x, gpre_ref[...]) * (1.0 + scale_ref[...]) + shift_ref[...]
    if has_pos or has_post:
        refs.pop(0)[...] = x
    if has_pre:
        refs.pop(0)[...] = u.astype(BF16)


def _post_pre(x, seq_len, *, pos=None, post=None, pre=None, tm=256):
    t, d = x.shape
    tm = min(tm, seq_len)
    assert seq_len % tm == 0 and t % tm == 0

    def mod_spec(m):
        if m.shape[0] == 1:
            return pl.BlockSpec((None, 1, d), lambda i: (0, 0, 0))
        return pl.BlockSpec((None, 1, d), lambda i: ((i * tm) // seq_len, 0, 0))

    row_spec = pl.BlockSpec((tm, d), lambda i: (i, 0))
    vec_spec = pl.BlockSpec((1, d), lambda i: (0, 0))
    args, in_specs = [x], [row_spec]
    if pos is not None:
        per_seq = seq_len // tm
        args.append(pos)
        in_specs.append(pl.BlockSpec((tm, d), lambda i: (i % per_seq, 0)))
    weight = 0.0
    if post is not None:
        o, gate, g_post, weight = post
        args += [o, gate, g_post]
        in_specs += [row_spec, mod_spec(gate), vec_spec]
    if pre is not None:
        shift, scale, g_pre = pre
        args += [shift, scale, g_pre]
        in_specs += [mod_spec(shift), mod_spec(scale), vec_spec]
    out_shape, out_specs = [], []
    new_x = pos is not None or post is not None
    if new_x:
        out_shape.append(jax.ShapeDtypeStruct((t, d), F32))
        out_specs.append(row_spec)
    if pre is not None:
        out_shape.append(jax.ShapeDtypeStruct((t, d), BF16))
        out_specs.append(row_spec)
    outs = pl.pallas_call(
        functools.partial(_post_pre_kernel, has_pos=pos is not None, has_post=post is not None,
                          has_pre=pre is not None, weight=weight),
        out_shape=out_shape,
        grid=(t // tm,),
        in_specs=in_specs,
        out_specs=out_specs,
        compiler_params=_params("parallel"),
        name="post_pre",
    )(*args)
    outs = list(outs)
    x_new = outs.pop(0) if new_x else None
    u = outs.pop(0) if pre is not None else None
    return x_new, u


def _mm_kernel(*refs, has_bias):
    refs = list(refs)
    x_ref, w_ref = refs.pop(0), refs.pop(0)
    b_ref = refs.pop(0) if has_bias else None
    o_ref = refs.pop(0)
    acc = jnp.dot(x_ref[...], w_ref[...].astype(BF16), preferred_element_type=F32)
    if has_bias:
        acc = acc + b_ref[...]
    o_ref[...] = acc.astype(o_ref.dtype)


def _row_tile_spec(tm, kdim):
    return pl.BlockSpec((tm, kdim), lambda i, j: (i, 0), pipeline_mode=pl.Buffered(1))


def _w_spec(w, w_index, tn, col_block0):
    lead = tuple(w_index)
    assert len(lead) == w.ndim - 2
    return pl.BlockSpec((None,) * len(lead) + (w.shape[-2], tn), lambda i, j: lead + (0, col_block0 + j))


def _matmul(x, w, *, w_index=(), col0=0, n=None, bias=None, out_dtype=F32, tm=1024, tn=1024,
            weight_stationary=False):
    t, kdim = x.shape
    n = w.shape[-1] - col0 if n is None else n
    assert w.shape[-2] == kdim
    tm, tn = _tile(t, tm), _tile(n, tn)
    if tn % _LANE or col0 % tn:
        w, w_index, col0 = w[tuple(w_index)][:, col0:col0 + n], (), 0
    args = [x, w]
    specs = [pl.BlockSpec((tm, kdim), lambda i, j: (i, 0)), _w_spec(w, w_index, tn, col0 // tn)]
    if bias is not None:
        args.append(bias.reshape(1, n).astype(F32))
        specs.append(pl.BlockSpec((1, tn), lambda i, j: (0, j)))
    specs.append(pl.BlockSpec((tm, tn), lambda i, j: (i, j)))
    grid = (t // tm, n // tn)
    if weight_stationary:
        grid = grid[::-1]
        specs = [pl.BlockSpec(s.block_shape, functools.partial(lambda f, j, i: f(i, j), s.index_map)) for s in specs]
    return pl.pallas_call(
        functools.partial(_mm_kernel, has_bias=bias is not None),
        out_shape=jax.ShapeDtypeStruct((t, n), out_dtype),
        grid=grid,
        in_specs=specs[:-1],
        out_specs=specs[-1],
        compiler_params=_params("parallel", "parallel"),
        name="matmul",
    )(*args)


def _mm_conv_kernel(*refs, seq, act, has_bias):
    refs = list(refs)
    x_ref, w_ref = refs.pop(0), refs.pop(0)
    b_ref = refs.pop(0) if has_bias else None
    cw_ref, cb_ref, o_ref = refs
    acc = jnp.dot(x_ref[...], w_ref[...].astype(BF16), preferred_element_type=F32)
    if has_bias:
        acc = acc + b_ref[...]
    tm = acc.shape[0]
    k_w = cw_ref.shape[0]
    pos = lax.rem(lax.broadcasted_iota(jnp.int32, (tm, 1), 0), seq)
    y = cb_ref[...] + acc * cw_ref[pl.ds(k_w // 2, 1), :]
    for k in range(k_w):
        d = k - k_w // 2
        if d != 0:
            shifted = pltpu.roll(acc, (-d) % tm, 0)
            inside = (pos + d >= 0) & (pos + d < seq)
            y = y + jnp.where(inside, shifted, 0.0) * cw_ref[pl.ds(k, 1), :]
    o_ref[...] = act(y).astype(o_ref.dtype)


def _matmul_conv(x, seq, w, w_index, col0, conv_w, conv_b, act, bias=None, tn=256, rows=1024):
    t, kdim = x.shape
    k_w, c = conv_w.shape
    tm = seq * max(1, min(rows, t) // seq)
    tn = _tile(c, tn)
    assert t % tm == 0 and col0 % tn == 0 and w.shape[-2] == kdim
    args = [x, w]
    in_specs = [_row_tile_spec(tm, kdim), _w_spec(w, w_index, tn, col0 // tn)]
    col_spec = lambda r: pl.BlockSpec((r, tn), lambda i, j: (0, j))
    if bias is not None:
        args.append(bias.reshape(1, c).astype(F32))
        in_specs.append(col_spec(1))
    args += [conv_w.astype(F32), conv_b.reshape(1, c).astype(F32)]
    in_specs += [col_spec(k_w), col_spec(1)]
    return pl.pallas_call(
        functools.partial(_mm_conv_kernel, seq=seq, act=act, has_bias=bias is not None),
        out_shape=jax.ShapeDtypeStruct((t, c), BF16),
        grid=(t // tm, c // tn),
        in_specs=in_specs,
        out_specs=pl.BlockSpec((tm, tn), lambda i, j: (i, j)),
        compiler_params=_params("parallel", "parallel"),
        name="matmul_conv",
    )(*args)


def _gate_up_kernel(u_ref, wg_ref, wu_ref, *rest):
    if len(rest) == 3:
        wd_ref, o_ref, wd16_ref = rest
        wd16_ref[...] = wd_ref[...].astype(BF16)
    else:
        o_ref, = rest
    u = u_ref[...]
    g = jnp.dot(u, wg_ref[...].astype(BF16), preferred_element_type=F32)
    v = jnp.dot(u, wu_ref[...].astype(BF16), preferred_element_type=F32)
    o_ref[...] = (_silu(g) * v).astype(o_ref.dtype)


def _gate_up(u, w_gate, w_up, w_index, w_down=None, tm=2048, tn=256):
    t, d = u.shape
    n = w_gate.shape[-1]
    tm, tn = _tile(t, tm), _tile(n, tn)
    nj = n // tn
    args = [u, w_gate, w_up]
    in_specs = [_row_tile_spec(tm, d), _w_spec(w_gate, w_index, tn, 0), _w_spec(w_up, w_index, tn, 0)]
    out_shape = [jax.ShapeDtypeStruct((t, n), BF16)]
    out_specs = [pl.BlockSpec((tm, tn), lambda i, j: (i, j))]
    if w_down is not None:
        steps = (t // tm) * nj
        d_out = w_down.shape[-1]
        slab = next(r for r in range(16, n + 1, 16) if n % r == 0 and n // r <= steps)
        lead = tuple(w_index)
        slab_of = lambda i, j: jnp.minimum(i * nj + j, n // slab - 1)
        args.append(w_down)
        in_specs.append(pl.BlockSpec((None,) * len(lead) + (slab, d_out), lambda i, j: lead + (slab_of(i, j), 0)))
        out_shape.append(jax.ShapeDtypeStruct((n, d_out), BF16))
        out_specs.append(pl.BlockSpec((slab, d_out), lambda i, j: (slab_of(i, j), 0)))
    outs = pl.pallas_call(
        _gate_up_kernel,
        out_shape=out_shape,
        grid=(t // tm, nj),
        in_specs=in_specs,
        out_specs=out_specs,
        compiler_params=_params("arbitrary", "arbitrary") if w_down is not None else _params("parallel", "parallel"),
        name="gate_up",
    )(*args)
    return tuple(outs) if w_down is not None else outs[0]


_HY_ROWS = 512


def _ssd_scan_kernel(*refs, has_h0, emit_state, hpg, p_dim):
    refs = list(refs)
    xs_ref, b_ref, c_ref, z_ref, dt_ref, dtb_ref, a_ref, dsk_ref, nw_ref = refs[:9]
    refs = refs[9:]
    h0_ref = refs.pop(0) if has_h0 else None
    y_ref = refs.pop(0)
    st_out_ref = refs.pop(0) if emit_state else None
    ybuf_ref, stf_ref, stb_ref, cs_ref, cst_ref, fac_ref = refs

    q = SSD_CHUNK
    seq, gp = xs_ref.shape
    nc = seq // q
    table_unroll = math.gcd(nc, 4)
    t_i = lax.broadcasted_iota(jnp.int32, (q, q), 0)
    s_i = lax.broadcasted_iota(jnp.int32, (q, q), 1)
    masks = (s_i <= t_i, s_i >= t_i)
    tris = (masks[0].astype(F32), masks[1].astype(F32))
    eye = (lax.broadcasted_iota(jnp.int32, (hpg, hpg), 0)
           == lax.broadcasted_iota(jnp.int32, (hpg, hpg), 1)).astype(F32)
    expand_bf = (lax.broadcasted_iota(jnp.int32, (hpg, gp), 1) // p_dim
                 == lax.broadcasted_iota(jnp.int32, (hpg, gp), 0)).astype(F32).astype(BF16)
    pair_lane = lax.broadcasted_iota(jnp.int32, (q, 2 * p_dim), 1)

    st_refs = (stf_ref, stb_ref)
    both = (0, 1)

    def chunk_rows(c):
        return pl.ds(pl.multiple_of(c * q, q), q)

    def decay_tables(chunks):
        jobs = [(c, d) for c in chunks for d in both]
        hs = [slice(d * hpg, (d + 1) * hpg) for d in both]
        dt = [_softplus(dt_ref[chunk_rows(c), :][:, hs[d]] + dtb_ref[...][:, hs[d]]) for c, d in jobs]
        a = [dt[k] * a_ref[...][:, hs[d]] for k, (c, d) in enumerate(jobs)]
        cs = [jnp.dot(tris[d], a[k], precision=HIGHEST, preferred_element_type=F32) for k, (c, d) in enumerate(jobs)]
        cs_t = [_nt_dot(eye, cs[k], precision=HIGHEST) for k in range(len(jobs))]
        end = [cs[k][q - 1:q, :] if d == 0 else cs[k][0:1, :] for k, (c, d) in enumerate(jobs)]
        for k, (c, d) in enumerate(jobs):
            cs_ref[d, c] = cs[k] * _LOG2E
            cst_ref[d, c] = cs_t[k] * _LOG2E
            dec = jnp.broadcast_to(jnp.exp(end[k]), (8, hpg))
            dec_hi = dec.astype(BF16)
            dec_lo = (dec - dec_hi.astype(F32)).astype(BF16)
            rows3 = jnp.concatenate([dt[k], dt[k] * jnp.exp(end[k] - cs[k]), jnp.exp(cs[k])], axis=0).astype(BF16)
            fac_ref[d, c] = jnp.concatenate([rows3, dec_hi, dec_lo], axis=0)

    def chunk_pair(cidx):
        rows = [chunk_rows(c) for c in cidx]
        cs = [cs_ref[d, cidx[d]] for d in both]
        cs_t = [cst_ref[d, cidx[d]] for d in both]
        fx = [jnp.dot(fac_ref[d, cidx[d]], expand_bf, preferred_element_type=F32) for d in both]
        chunk_decay = [fx[d][3 * q:3 * q + 1] + fx[d][3 * q + 8:3 * q + 9] for d in both]
        bm = [b_ref[rows[d], :] for d in both]
        cm = [c_ref[rows[d], :] for d in both]
        cb = [_nt_dot(cm[d], bm[d]) for d in both]
        xs = [xs_ref[rows[d], :].astype(F32) for d in both]
        xg = [(xs[d] * fx[d][:q]).astype(BF16) for d in both]
        xd = [(xs[d] * fx[d][q:2 * q]).astype(BF16) for d in both]
        st = [st_refs[d][...] for d in both]
        y_off = [jnp.dot(cm[d], st[d].astype(BF16), preferred_element_type=F32) * fx[d][2 * q:3 * q] for d in both]
        b_t = [bm[d].astype(F32).T.astype(BF16) for d in both]
        for d in both:
            st_refs[d][...] = st[d] * chunk_decay[d] + jnp.dot(b_t[d], xd[d], preferred_element_type=F32)
        pieces = ([], [])
        for p in range(hpg // 2):
            for d in both:
                xg_pair = xg[d][:, 2 * p * p_dim:(2 * p + 2) * p_dim]
                outs = []
                for r in (2 * p, 2 * p + 1):
                    seg = cs[d][:, r:r + 1] - cs_t[d][r:r + 1, :]
                    m = (cb[d] * jnp.exp2(jnp.where(masks[d], seg, -jnp.inf))).astype(BF16)
                    outs.append(jnp.dot(m, xg_pair, preferred_element_type=F32))
                pieces[d].append(jnp.where(pair_lane < p_dim, outs[0], outs[1]))
        return [jnp.concatenate(pieces[d], axis=1) + y_off[d] for d in both]

    if has_h0:
        stf_ref[...] = h0_ref[0]
        stb_ref[...] = h0_ref[1]
    else:
        stf_ref[...] = jnp.zeros_like(stf_ref)
        stb_ref[...] = jnp.zeros_like(stb_ref)

    def skip_body(i, carry):
        rows = pl.ds(pl.multiple_of(i * q, q), q)
        ybuf_ref[rows, :] = xs_ref[rows, :].astype(F32) * dsk_ref[...]
        return carry

    lax.fori_loop(0, nc, skip_body, 0)

    def tables_body(i, carry):
        decay_tables([i * table_unroll + k for k in range(table_unroll)])
        return carry

    lax.fori_loop(0, nc // table_unroll, tables_body, 0)

    def gate_norm(rows, y):
        y = y * _silu(z_ref[rows, :].astype(F32))
        y_ref[rows, :] = _rms(y, nw_ref[...]).astype(y_ref.dtype)

    def first_half(i, carry):
        yf, yb = chunk_pair((i, nc - 1 - i))
        ybuf_ref[chunk_rows(i), :] += yf
        ybuf_ref[chunk_rows(nc - 1 - i), :] += yb
        return carry

    def second_half(i, carry):
        yf, yb = chunk_pair((i, nc - 1 - i))
        gate_norm(chunk_rows(i), ybuf_ref[chunk_rows(i), :] + yf)
        gate_norm(chunk_rows(nc - 1 - i), ybuf_ref[chunk_rows(nc - 1 - i), :] + yb)
        return carry

    lax.fori_loop(0, nc // 2, first_half, 0)
    lax.fori_loop(nc // 2, nc, second_half, 0)

    if emit_state:
        st_out_ref[0] = stf_ref[...]
        st_out_ref[1] = stb_ref[...]


def _ssd_scan(z, xbc, dt, dt_bias, a_head, d_skip, norm_w, h0, emit_state, d_inner):
    b, l, _ = z.shape
    g = SSD_GROUPS
    gp = d_inner // g
    n = D_STATE
    hpg = gp // SSD_HEAD_DIM
    assert gp % _LANE == 0 and l % SSD_CHUNK == 0 and n == _LANE
    nc = l // SSD_CHUNK
    assert nc % 2 == 0
    xoff = d_inner // n
    in_specs = [
        pl.BlockSpec((None, l, gp), lambda i, j: (i, 0, j)),
        pl.BlockSpec((None, l, n), lambda i, j: (i, 0, xoff + j)),
        pl.BlockSpec((None, l, n), lambda i, j: (i, 0, xoff + g + j)),
        pl.BlockSpec((None, l, gp), lambda i, j: (i, 0, j)),
        pl.BlockSpec((None, None, l, 2 * hpg), lambda i, j: (i, j, 0, 0)),
        pl.BlockSpec((None, 1, 2 * hpg), lambda i, j: (j, 0, 0)),
        pl.BlockSpec((None, 1, 2 * hpg), lambda i, j: (j, 0, 0)),
        pl.BlockSpec((1, gp), lambda i, j: (0, j)),
        pl.BlockSpec((1, gp), lambda i, j: (0, j)),
    ]
    args = [xbc, xbc, xbc, z, dt, dt_bias, a_head, d_skip, norm_w]
    state_spec = pl.BlockSpec((None, 2, None, n, gp), lambda i, j: (i, 0, j, 0, 0))
    if h0 is not None:
        in_specs.append(state_spec)
        args.append(h0)
    out_shape = [jax.ShapeDtypeStruct((b, l, d_inner), BF16)]
    out_specs = [pl.BlockSpec((None, l, gp), lambda i, j: (i, 0, j))]
    if emit_state:
        out_shape.append(jax.ShapeDtypeStruct((b, 2, g, n, gp), F32))
        out_specs.append(state_spec)
    outs = pl.pallas_call(
        functools.partial(_ssd_scan_kernel, has_h0=h0 is not None, emit_state=emit_state,
                          hpg=hpg, p_dim=SSD_HEAD_DIM),
        out_shape=out_shape,
        grid=(b, g),
        in_specs=in_specs,
        out_specs=out_specs,
        scratch_shapes=[pltpu.VMEM((l, gp), F32), pltpu.VMEM((n, gp), F32), pltpu.VMEM((n, gp), F32),
                        pltpu.VMEM((2, nc, SSD_CHUNK, hpg), F32), pltpu.VMEM((2, nc, hpg, SSD_CHUNK), F32),
                        pltpu.VMEM((2, nc, 3 * SSD_CHUNK + 16, hpg), BF16)],
        compiler_params=_params("parallel", "parallel"),
        name="ssd_scan",
    )(*args)
    return outs[0], (outs[1] if emit_state else None)


def _ssd_mixer(u, bsz, seq, h0, emit_state, w_in, w_out, li, conv_w, conv_b, dt_bias, a_log, d_skip, norm_w):
    d_inner = w_out.shape[1]
    heads = d_inner // SSD_HEAD_DIM
    g = SSD_GROUPS
    hpg = heads // g
    gp = d_inner // g
    n_main = w_in.shape[-1] - 2 * heads
    z = _matmul(u, w_in, w_index=(li,), n=d_inner, out_dtype=BF16, tn=512).reshape(bsz, seq, d_inner)
    xbc = _matmul_conv(u, seq, w_in, (li,), d_inner, conv_w, conv_b, _silu).reshape(bsz, seq, n_main - d_inner)
    dt = _matmul(u, w_in, w_index=(li,), col0=n_main, out_dtype=F32, tn=2 * heads)
    dt = dt.reshape(bsz, seq, 2, g, hpg).transpose(0, 3, 1, 2, 4).reshape(bsz, g, seq, 2 * hpg)
    per_group = lambda v: v.astype(F32).reshape(2, g, hpg).transpose(1, 0, 2).reshape(g, 1, 2 * hpg)
    if h0 is not None:
        h0 = h0.astype(F32).reshape(bsz, 2, g, hpg, SSD_HEAD_DIM, D_STATE)
        h0 = h0.transpose(0, 1, 2, 5, 3, 4).reshape(bsz, 2, g, D_STATE, gp)
    y, st = _ssd_scan(z, xbc, dt, per_group(dt_bias), per_group(-jnp.exp(a_log.astype(F32))),
                      jnp.repeat(d_skip.astype(F32), SSD_HEAD_DIM).reshape(1, d_inner),
                      norm_w.astype(F32).reshape(1, d_inner), h0, emit_state, d_inner)
    out = _matmul(y.reshape(bsz * seq, d_inner), w_out, w_index=(li,), out_dtype=BF16, tm=512, tn=512)
    if st is not None:
        st = st.reshape(bsz, 2, g, D_STATE, hpg, SSD_HEAD_DIM).transpose(0, 1, 2, 4, 5, 3)
        st = st.reshape(bsz, 2, heads, SSD_HEAD_DIM, D_STATE)
    return out, st


def _hy_mlp_kernel(bands_ref, w1t_ref, w1c_ref, w1s_ref, b1_ref, w2_ref, b2_ref, w3_ref, b3_ref, fq_ref,
                   o_ref, *, n):
    idx = lax.broadcasted_iota(jnp.int32, (n, 1), 0).astype(F32)
    hdot = functools.partial(jnp.dot, precision=HIGHEST, preferred_element_type=F32)
    for direction in range(2):
        pos = idx if direction == 0 else (n - 1.0) - idx
        t = pos / (n - 1.0)
        ang = (2.0 * math.pi * pos / n) * bands_ref[...]
        h = t * w1t_ref[...] + hdot(jnp.cos(ang), w1c_ref[...]) - hdot(jnp.sin(ang), w1s_ref[...])
        h = jnp.sin(fq_ref[0:1, :] * (h + b1_ref[...]))
        h = jnp.sin(fq_ref[1:2, :] * (hdot(h, w2_ref[...]) + b2_ref[...]))
        h = jnp.sin(fq_ref[2:3, :] * (hdot(h, w3_ref[...]) + b3_ref[...]))
        o_ref[direction] = h


def _hy_filter_kernel(h_ref, w_ref, delta_ref, o_ref, *, n):
    direction = pl.program_id(0) % 2
    idx = lax.broadcasted_iota(jnp.int32, (n, 1), 0).astype(F32)
    pos = jnp.where(direction == 0, idx, (n - 1.0) - idx)
    t = pos / (n - 1.0)
    k = jnp.dot(h_ref[...], w_ref[...], precision=HIGHEST, preferred_element_type=F32)
    k = k * jnp.exp(-t * delta_ref[...])
    o_ref[...] = k / (jnp.sum(jnp.abs(k), axis=0, keepdims=True) + RMS_EPS)


def _hy_spectrum_kernel(kf_ref, kb_ref, c_ref, s_ref, kp_ref, kq_ref, kn_ref, kf16_ref, kb16_ref, *, n, rchunk):
    dot = functools.partial(jnp.dot, preferred_element_type=F32)
    chunks = [slice(r0, r0 + rchunk) for r0 in range(0, n, rchunk)]
    alt = _alt_sign(0, rchunk)
    sign_n = 1.0 if n % 2 == 0 else -1.0
    nyq = jnp.zeros((1, kf_ref.shape[1]), F32)
    for rs in chunks:
        kf, kb = kf_ref[rs, :], kb_ref[rs, :]
        kf16_ref[rs, :] = kf.astype(BF16)
        kb16_ref[rs, :] = kb.astype(BF16)
        nyq = nyq + jnp.sum(alt * kf, axis=0, keepdims=True) + sign_n * jnp.sum(alt * kb, axis=0, keepdims=True)
    kn_ref[...] = nyq / (2.0 * n)
    for fs in chunks:
        freq = fs.start + lax.broadcasted_iota(jnp.int32, (rchunk, 1), 0)
        wgt = jnp.where(freq == 0, 1.0, 2.0) / (2.0 * n)
        kp_ref[fs, :] = (dot(c_ref[fs, :], kf16_ref[...]) + alt * dot(c_ref[fs, :], kb16_ref[...])) * wgt
        kq_ref[fs, :] = (dot(s_ref[fs, :], kf16_ref[...]) + alt * dot(s_ref[fs, :], kb16_ref[...])) * wgt


def _alt_sign(r0, rows):
    row = lax.broadcasted_iota(jnp.int32, (rows, 1), 0)
    return jnp.where(row % 2 == 0, 1.0, -1.0)


def _hy_conv_kernel(x1_ref, x2_ref, v_ref, kp_ref, kq_ref, kn_ref, bias_ref, c_ref, s_ref, o_ref,
                    z_ref, z16_ref, p2_ref, q2_ref, *, n, rchunk):
    dot = functools.partial(jnp.dot, preferred_element_type=F32)
    chunks = [slice(r0, r0 + rchunk) for r0 in range(0, n, rchunk)]
    alt = _alt_sign(0, rchunk)
    gate_refs = (x1_ref, x2_ref)
    for o in range(2):
        nyq = jnp.zeros((1, z_ref.shape[1]), F32)
        for rs in chunks:
            z = v_ref[rs, :].astype(F32) if o == 0 else z_ref[rs, :]
            if o == 0:
                z_ref[rs, :] = z
            z16_ref[rs, :] = z.astype(BF16)
            nyq = nyq + jnp.sum(alt * z, axis=0, keepdims=True)
        nyq = nyq * kn_ref[o]
        for fs in chunks:
            p = dot(c_ref[fs, :], z16_ref[...])
            q = dot(s_ref[fs, :], z16_ref[...])
            kp, kq = kp_ref[o, fs, :], kq_ref[o, fs, :]
            p2_ref[fs, :] = (p * kp - q * kq).astype(BF16)
            q2_ref[fs, :] = (p * kq + q * kp).astype(BF16)
        for rs in chunks:
            zc = dot(c_ref[rs, :], p2_ref[...]) + dot(s_ref[rs, :], q2_ref[...]) + alt * nyq
            z = gate_refs[o][rs, :].astype(F32) * (zc + z_ref[rs, :] * bias_ref[pl.ds(o, 1), :])
            if o == 0:
                z_ref[rs, :] = z
            else:
                o_ref[rs, :] = z.astype(o_ref.dtype)


def _dft_tables(n):
    f = lax.broadcasted_iota(jnp.int32, (n, n), 0)
    s = lax.broadcasted_iota(jnp.int32, (n, n), 1)
    ang = ((f * s) % (2 * n)).astype(F32) * (math.pi / n)
    return jnp.cos(ang).astype(BF16), jnp.sin(ang).astype(BF16)


def _resident(shape):
    return pl.BlockSpec(shape, lambda *_: (0,) * len(shape), pipeline_mode=pl.Buffered(1))


def _hyena_spectra(n, d, f_w1, f_b1, f_w2, f_b2, f_w3, f_b3, f_freq, f_w_out, tables, tc=512):
    fw = f_w1.shape[1]
    bands = jnp.linspace(1e-4, HY_BANDS - 1, HY_BANDS, dtype=F32).reshape(1, HY_BANDS)
    f32 = lambda a: a.astype(F32)
    row = lambda a: f32(a).reshape(1, -1)
    hdn = pl.pallas_call(
        functools.partial(_hy_mlp_kernel, n=n),
        out_shape=jax.ShapeDtypeStruct((2, n, fw), F32),
        name="hy_mlp",
    )(bands, f32(f_w1[0:1]), f32(f_w1[1:1 + HY_BANDS]), f32(f_w1[1 + HY_BANDS:]), row(f_b1),
      f32(f_w2), row(f_b2), f32(f_w3), row(f_b3), f32(f_freq))
    deltas = jnp.abs(jnp.linspace(math.log(HY_DECAY_TARGET) / HY_SLOW_DECAY,
                                  math.log(HY_DECAY_TARGET) / HY_FAST_DECAY, d, dtype=F32)).reshape(1, d)
    tc = min(tc, d)
    nj = d // tc
    k = pl.pallas_call(
        functools.partial(_hy_filter_kernel, n=n),
        out_shape=jax.ShapeDtypeStruct((4, n, d), F32),
        grid=(4, nj),
        in_specs=[pl.BlockSpec((None, n, fw), lambda i, j: (i % 2, 0, 0)),
                  pl.BlockSpec((fw, tc), lambda i, j: (0, i * nj + j)),
                  pl.BlockSpec((1, tc), lambda i, j: (0, j))],
        out_specs=pl.BlockSpec((None, n, tc), lambda i, j: (i, 0, j)),
        compiler_params=_params("parallel", "parallel"),
        name="hy_filter",
    )(hdn, f32(f_w_out), deltas)
    cos_t, sin_t = tables
    tcs = min(256, d)
    kp, kq, kn = pl.pallas_call(
        functools.partial(_hy_spectrum_kernel, n=n, rchunk=min(_HY_ROWS, n)),
        out_shape=[jax.ShapeDtypeStruct((2, n, d), F32), jax.ShapeDtypeStruct((2, n, d), F32),
                   jax.ShapeDtypeStruct((2, 1, d), F32)],
        grid=(2, d // tcs),
        in_specs=[pl.BlockSpec((None, n, tcs), lambda o, j: (2 * o, 0, j)),
                  pl.BlockSpec((None, n, tcs), lambda o, j: (2 * o + 1, 0, j)),
                  _resident((n, n)), _resident((n, n))],
        out_specs=[pl.BlockSpec((None, n, tcs), lambda o, j: (o, 0, j)),
                   pl.BlockSpec((None, n, tcs), lambda o, j: (o, 0, j)),
                   pl.BlockSpec((None, 1, tcs), lambda o, j: (o, 0, j))],
        scratch_shapes=[pltpu.VMEM((n, tcs), BF16), pltpu.VMEM((n, tcs), BF16)],
        compiler_params=_params("parallel", "parallel"),
        name="hy_spectrum",
    )(k, k, cos_t, sin_t)
    return kp, kq, kn


_HY_BLOCK = 2048 * 256


def _hyena_conv(proj, spectra, bias, tables):
    b, n, d3 = proj.shape
    d = d3 // 3
    tc = _tile(d, max(2 * _LANE, _HY_BLOCK // n))
    nj = d // tc
    kp, kq, kn = spectra
    cos_t, sin_t = tables
    part = lambda p: pl.BlockSpec((None, n, tc), lambda j, i: (i, 0, p * nj + j))
    spec = pl.BlockSpec((2, n, tc), lambda j, i: (0, 0, j), pipeline_mode=pl.Buffered(1))
    return pl.pallas_call(
        functools.partial(_hy_conv_kernel, n=n, rchunk=min(_HY_ROWS, n)),
        out_shape=jax.ShapeDtypeStruct((b, n, d), BF16),
        grid=(nj, b),
        in_specs=[part(0), part(1), part(2), spec, spec,
                  pl.BlockSpec((2, 1, tc), lambda j, i: (0, 0, j)),
                  pl.BlockSpec((2, tc), lambda j, i: (0, j)),
                  _resident((n, n)), _resident((n, n))],
        out_specs=pl.BlockSpec((None, n, tc), lambda j, i: (i, 0, j)),
        scratch_shapes=[pltpu.VMEM((n, tc), F32), pltpu.VMEM((n, tc), BF16),
                        pltpu.VMEM((n, tc), BF16), pltpu.VMEM((n, tc), BF16)],
        compiler_params=_params("parallel", "parallel"),
        name="hy_conv",
    )(proj, proj, proj, kp, kq, kn, bias.astype(F32), cos_t, sin_t)


def _hyena_mixer(u, bsz, seq, w_in, w_out, li, b_in, short_w, short_b, spectra, bias, b_out, tables):
    proj = _matmul_conv(u, seq, w_in, (li,), 0, short_w, short_b, lambda y: y, bias=b_in)
    z = _hyena_conv(proj.reshape(bsz, seq, -1), spectra, bias, tables)
    return _matmul(z.reshape(bsz * seq, -1), w_out, w_index=(li,), bias=b_out, out_dtype=BF16, tn=512)


def _latent_pos_embed(n_tok, d):
    rows = n_tok // GRID_W
    r = jnp.repeat(jnp.arange(rows, dtype=F32), GRID_W)
    col = jnp.tile(jnp.arange(GRID_W, dtype=F32), rows)
    quarter = d // 4
    omega = 1.0 / (10000.0 ** (jnp.arange(quarter, dtype=F32) / quarter))
    ar = r[:, None] * omega[None]
    ac = col[:, None] * omega[None]
    return jnp.concatenate([jnp.sin(ar), jnp.cos(ar), jnp.sin(ac), jnp.cos(ac)], axis=-1)


def kernel(x_prompt, x_sample, state_ssd, c, c_ctx, w_mod, b_mod, g_pre, g_post, ffn_w_gate, ffn_w_up, ffn_w_down, ssd_w_in, ssd_conv_w, ssd_conv_b, ssd_dt_bias, ssd_a_log, ssd_d, ssd_norm, ssd_w_out, hy_w_in, hy_b_in, hy_short_w, hy_short_b, hy_f_w1, hy_f_b1, hy_f_w2, hy_f_b2, hy_f_w3, hy_f_b3, hy_f_freq, hy_f_w_out, hy_bias, hy_w_out, hy_b_out):
    depth, d, _ = w_mod.shape
    ssd_w_out16 = ssd_w_out.astype(BF16)

    streams = []
    for x, rows in ((x_prompt, slice(0, 1)), (x_sample, slice(1, 1 + c.shape[0]))):
        bsz, seq, _ = x.shape
        streams.append(dict(bsz=bsz, seq=seq, x=x.reshape(bsz * seq, d), rows=rows))
    ctx, lat = streams

    cvec = jnp.concatenate([c_ctx[None], c], axis=0)
    n_cond = cvec.shape[0]
    cvec = jnp.pad(cvec, ((0, -n_cond % 8), (0, 0)))
    mod = _modulation(cvec, w_mod, b_mod).reshape(depth, -1, 3 * N_SUB, 1, d)

    def mod_vec(i, s, j, which):
        return mod[i, s["rows"], 3 * j + which]

    def pre_args(i, s, j):
        return (mod_vec(i, s, j, 0), mod_vec(i, s, j, 1), g_pre[i, j].reshape(1, d))

    def post_args(i, s, j, o, weight):
        return (o, mod_vec(i, s, j, 2), g_post[i, j].reshape(1, d), weight)

    tables = {s["seq"]: _dft_tables(s["seq"]) for s in streams} if depth > 1 else {}

    pos = _latent_pos_embed(lat["seq"], d)
    _, ctx["u"] = _post_pre(ctx["x"], ctx["seq"], pre=pre_args(0, ctx, 0))
    lat["x"], lat["u"] = _post_pre(lat["x"], lat["seq"], pos=pos, pre=pre_args(0, lat, 0))

    new_states = []
    for i in range(depth):
        kind, li = i % 2, i // 2
        for j in range(N_SUB):
            if j == 1 and kind == 0:
                outs = []
                for s, h0, emit in ((ctx, None, True), (lat, state_ssd[:, li], False)):
                    o, st = _ssd_mixer(s["u"], s["bsz"], s["seq"], h0, emit, ssd_w_in, ssd_w_out16, li,
                                       ssd_conv_w[li], ssd_conv_b[li], ssd_dt_bias[li], ssd_a_log[li],
                                       ssd_d[li], ssd_norm[li])
                    outs.append(o)
                    if emit:
                        new_states.append(st.astype(x_prompt.dtype))
                weight = 1.0
            elif j == 1:
                outs = []
                for s in streams:
                    spectra = _hyena_spectra(s["seq"], d, hy_f_w1[li], hy_f_b1[li], hy_f_w2[li], hy_f_b2[li],
                                             hy_f_w3[li], hy_f_b3[li], hy_f_freq[li], hy_f_w_out[li],
                                             tables[s["seq"]])
                    outs.append(_hyena_mixer(s["u"], s["bsz"], s["seq"], hy_w_in, hy_w_out, li, hy_b_in[li],
                                             hy_short_w[li], hy_short_b[li], spectra, hy_bias[li], hy_b_out[li],
                                             tables[s["seq"]]))
                weight = 1.0
            else:
                slot = (i, j // 2)
                h_ctx, w_down16 = _gate_up(ctx["u"], ffn_w_gate, ffn_w_up, slot, w_down=ffn_w_down)
                hidden = (h_ctx, _gate_up(lat["u"], ffn_w_gate, ffn_w_up, slot))
                outs = [_matmul(h, w_down16, out_dtype=BF16, tm=512, tn=512, weight_stationary=True)
                        for h in hidden]
                weight = 0.5
            nxt = (i, j + 1) if j + 1 < N_SUB else ((i + 1, 0) if i + 1 < depth else None)
            for s, o in zip(streams, outs):
                pre = pre_args(nxt[0], s, nxt[1]) if nxt is not None else None
                s["x"], s["u"] = _post_pre(s["x"], s["seq"], post=post_args(i, s, j, o, weight), pre=pre)

    y_prompt = ctx["x"].reshape(x_prompt.shape)
    y_sample = lat["x"].reshape(x_sample.shape)
    return (y_prompt, y_sample, jnp.stack(new_states, axis=1))
```
